```python
import jax, jax.numpy as jnp
from jax import lax
import numpy as np

D_MODEL = 1024
BATCH = 32
SEQ = 2048
DEPTH = 2

DN_HEAD_DIM = 128
DN_HEADS = D_MODEL // (2 * DN_HEAD_DIM)
DN_WIDTH = DN_HEADS * DN_HEAD_DIM
DN_CONV = 5
DN_CHUNK = 64
HG_EXPAND = 128
HG_HEAD_DIM = 128
HG_HEADS = D_MODEL // (2 * HG_HEAD_DIM)
HG_KEY_WIDTH = HG_HEADS * HG_EXPAND
HG_VAL_WIDTH = HG_HEADS * HG_HEAD_DIM
HG_CHUNK = 32
D_FF = 2816
FFN_CONV = 3
EPS = 1e-6

IN_SIZES = (3 * DN_WIDTH, DN_WIDTH, 2 * DN_HEADS, 2 * DN_HEADS,
            HG_KEY_WIDTH, HG_KEY_WIDTH, HG_KEY_WIDTH, HG_VAL_WIDTH, HG_VAL_WIDTH,
            D_MODEL, D_MODEL)
IN_COLS = 4 * DN_WIDTH + 4 * DN_HEADS + 3 * HG_KEY_WIDTH + 2 * HG_VAL_WIDTH + 2 * D_MODEL

kernel_name = 'hybrid_deltanet_hgrn2_encoder'


def _split(x, sizes):
    idx = np.cumsum(np.array(sizes))[:-1].tolist()
    return jnp.split(x, idx, axis=-1)


def _rmsnorm(x, w):
    xf = x.astype(jnp.float32)
    y = xf * lax.rsqrt(jnp.mean(xf * xf, axis=-1, keepdims=True) + EPS)
    return (y * w.astype(jnp.float32)).astype(x.dtype)


def _gated_rmsnorm(o, z, w):
    y = o * lax.rsqrt(jnp.mean(o * o, axis=-1, keepdims=True) + EPS)
    return y * w.astype(jnp.float32) * jax.nn.silu(z)


def _l2norm(x):
    return x * lax.rsqrt(jnp.sum(x * x, axis=-1, keepdims=True) + EPS)


def _dwconv(x, w):
    return lax.conv_general_dilated(x, w[:, None, :].astype(x.dtype), window_strides=(1,),
                                    padding='SAME', dimension_numbers=('NWC', 'WIO', 'NWC'),
                                    feature_group_count=x.shape[-1])


def _to_heads(x, n):
    b, s, _ = x.shape
    return x.reshape(b, s, n, -1).transpose(0, 2, 1, 3)


def _gated_delta_rule(q, k, v, beta, g):
    b, h, s, dk = q.shape
    dv = v.shape[-1]
    c = DN_CHUNK
    n = s // c
    q = q.reshape(b, h, n, c, dk)
    k = k.reshape(b, h, n, c, dk)
    v = v.reshape(b, h, n, c, dv)
    beta = beta.reshape(b, h, n, c)
    gc = jnp.cumsum(g.reshape(b, h, n, c), axis=-1)
    incl = jnp.tril(jnp.ones((c, c), dtype=bool))
    strict = jnp.tril(jnp.ones((c, c), dtype=bool), k=-1)
    decay = jnp.exp(jnp.where(incl, gc[..., :, None] - gc[..., None, :], -jnp.inf))
    kb = k * beta[..., None]
    a = jnp.where(strict, jnp.einsum('bhntd,bhnsd->bhnts', kb, k) * decay, 0.0)
    eye = jnp.eye(c, dtype=q.dtype)
    t_inv = lax.linalg.triangular_solve(a + eye, jnp.broadcast_to(eye, a.shape),
                                        left_side=True, lower=True, unit_diagonal=True)
    u = jnp.einsum('bhnts,bhnsv->bhntv', t_inv, v * beta[..., None])
    w = jnp.einsum('bhnts,bhnsd->bhntd', t_inv, kb * jnp.exp(gc)[..., None])
    qk = jnp.einsum('bhntd,bhnsd->bhnts', q, k) * decay
    qg = q * jnp.exp(gc)[..., None]
    kd = k * jnp.exp(gc[..., -1:] - gc)[..., None]
    gl = jnp.exp(gc[..., -1])
    xs = (jnp.moveaxis(u, 2, 0), jnp.moveaxis(w, 2, 0), jnp.moveaxis(qk, 2, 0),
          jnp.moveaxis(qg, 2, 0), jnp.moveaxis(kd, 2, 0), jnp.moveaxis(gl, 2, 0))

    def step(state, inp):
        u_n, w_n, qk_n, qg_n, kd_n, gl_n = inp
        v_new = u_n - jnp.einsum('bhtd,bhdv->bhtv', w_n, state)
        o = jnp.einsum('bhtd,bhdv->bhtv', qg_n, state) + jnp.einsum('bhts,bhsv->bhtv', qk_n, v_new)
        state = state * gl_n[..., None, None] + jnp.einsum('bhsd,bhsv->bhdv', kd_n, v_new)
        return state, o

    state0 = jnp.zeros((b, h, dk, dv), dtype=q.dtype)
    _, o = lax.scan(step, state0, xs)
    return jnp.moveaxis(o, 0, 2).reshape(b, h, s, dv)


def _hgrn2_scan(q, k, i, logf):
    b, h, s, dk = q.shape
    dv = i.shape[-1]
    c = HG_CHUNK
    n = s // c
    q = jnp.moveaxis(q.reshape(b, h, n, c, dk), 2, 0)
    k = jnp.moveaxis(k.reshape(b, h, n, c, dk), 2, 0)
    i = jnp.moveaxis(i.reshape(b, h, n, c, dv), 2, 0)
    bc = jnp.moveaxis(jnp.cumsum(logf.reshape(b, h, n, c, dk), axis=-2), 2, 0)
    incl = jnp.tril(jnp.ones((c, c), dtype=bool))[:, :, None]

    def step(state, inp):
        q_n, k_n, i_n, b_n = inp
        dec = jnp.exp(jnp.where(incl, b_n[:, :, :, None, :] - b_n[:, :, None, :, :], -jnp.inf))
        att = jnp.einsum('bhtd,bhsd,bhtsd->bhts', q_n, k_n, dec)
        o = (jnp.einsum('bhtd,bhdv->bhtv', q_n * jnp.exp(b_n), state)
             + jnp.einsum('bhts,bhsv->bhtv', att, i_n))
        last = b_n[:, :, -1:, :]
        state = (state * jnp.exp(last[:, :, 0, :])[..., None]
                 + jnp.einsum('bhsd,bhsv->bhdv', k_n * jnp.exp(last - b_n), i_n))
        return state, o

    state0 = jnp.zeros((b, h, dk, dv), dtype=q.dtype)
    _, o = lax.scan(step, state0, (q, k, i, bc))
    return jnp.moveaxis(o, 0, 2).reshape(b, h, s, dv)


def _flip(t):
    return jnp.flip(t, axis=2)


def _mixer(h, w_in, dn_conv, dn_a_log, dn_dt_bias, dn_norm, lb, hg_norm, w_branch_dn, w_branch_hg, w_out):
    b, s, _ = h.shape
    f32 = jnp.float32
    p = h @ w_in
    (qkv, z, beta_r, a_r, hq, hf_fwd, hf_bwd, hi, hgate, gate_dn, gate_hg) = _split(p, IN_SIZES)

    qkv = jax.nn.silu(_dwconv(qkv, dn_conv)).astype(f32)
    q, k, v = _split(qkv, (DN_WIDTH, DN_WIDTH, DN_WIDTH))
    q = _l2norm(_to_heads(q, DN_HEADS)) * (DN_HEAD_DIM ** -0.5)
    k = _l2norm(_to_heads(k, DN_HEADS))
    v = _to_heads(v, DN_HEADS)
    beta = jax.nn.sigmoid(beta_r.astype(f32)).reshape(b, s, 2, DN_HEADS).transpose(2, 0, 3, 1)
    g = (-jnp.exp(dn_a_log.astype(f32))
         * jax.nn.softplus(a_r.astype(f32).reshape(b, s, 2, DN_HEADS) + dn_dt_bias.astype(f32)))
    g = g.transpose(2, 0, 3, 1)
    o_f = _gated_delta_rule(q, k, v, beta[0], g[0])
    o_b = _flip(_gated_delta_rule(_flip(q), _flip(k), _flip(v), _flip(beta[1]), _flip(g[1])))
    o_dn = (o_f + o_b).transpose(0, 2, 1, 3)
    o_dn = _gated_rmsnorm(o_dn, z.astype(f32).reshape(b, s, DN_HEADS, DN_HEAD_DIM), dn_norm)
    o_dn = o_dn.reshape(b, s, DN_WIDTH).astype(h.dtype)

    lb = lb.astype(f32)
    qh = _to_heads(jax.nn.silu(hq.astype(f32)), HG_HEADS) * (HG_EXPAND ** -0.5)
    ih = _to_heads(hi.astype(f32), HG_HEADS)
    outs = []
    for d, fr in enumerate((hf_fwd, hf_bwd)):
        fr = fr.astype(f32)
        logf = jax.nn.log_sigmoid(fr) + jnp.log1p(lb[d] * jnp.exp(-fr))
        kk = (1.0 - lb[d]) * jax.nn.sigmoid(-fr)
        outs.append((_to_heads(kk, HG_HEADS), _to_heads(logf, HG_HEADS)))
    o_hf = _hgrn2_scan(qh, outs[0][0], ih, outs[0][1])
    o_hb = _flip(_hgrn2_scan(_flip(qh), _flip(outs[1][0]), _flip(ih), _flip(outs[1][1])))
    o_hg = (o_hf + o_hb).transpose(0, 2, 1, 3)
    o_hg = _gated_rmsnorm(o_hg, hgate.astype(f32).reshape(b, s, HG_HEADS, HG_HEAD_DIM), hg_norm)
    o_hg = o_hg.reshape(b, s, HG_VAL_WIDTH).astype(h.dtype)

    merged = (jax.nn.sigmoid(gate_dn) * (o_dn @ w_branch_dn)
              + jax.nn.sigmoid(gate_hg) * (o_hg @ w_branch_hg))
    return merged @ w_out


def _conv_ffn(h, w_up, ffn_conv, ffn_conv_bias, w_down):
    gate, up = _split(h @ w_up, (D_FF, D_FF))
    gate = _dwconv(gate, ffn_conv) + ffn_conv_bias
    return (jax.nn.silu(gate) * up) @ w_down


def _log(v):
    return float(np.log(v))


def setup_inputs(seed: int = 0) -> dict:
    key = jax.random.key(seed)
    ks = jax.random.split(key, 20)
    f32 = jnp.float32
    nrm = lambda k, shape, scale: jax.random.normal(k, shape, f32) * scale
    dt = jnp.exp(jax.random.uniform(ks[5], (DEPTH, 2, DN_HEADS), f32)
                 * (_log(0.1) - _log(0.001)) + _log(0.001))
    return {
        'x': nrm(ks[0], (BATCH, SEQ, D_MODEL), 1.0),
        'mix_norm': 1.0 + nrm(ks[1], (DEPTH, D_MODEL), 0.02),
        'w_in': nrm(ks[2], (DEPTH, D_MODEL, IN_COLS), D_MODEL ** -0.5),
        'dn_conv': nrm(ks[3], (DEPTH, DN_CONV, 3 * DN_WIDTH), DN_CONV ** -0.5),
        'dn_a_log': jnp.log(jax.random.uniform(ks[4], (DEPTH, 2, DN_HEADS), f32, 1.0, 16.0)),
        'dn_dt_bias': dt + jnp.log(-jnp.expm1(-dt)),
        'dn_norm': 1.0 + nrm(ks[6], (DEPTH, DN_HEAD_DIM), 0.02),
        'hg_lb_logits': nrm(ks[7], (DEPTH, 2, HG_KEY_WIDTH), 1.0),
        'hg_norm': 1.0 + nrm(ks[8], (DEPTH, HG_HEAD_DIM), 0.02),
        'w_branch_dn': nrm(ks[9], (DEPTH, DN_WIDTH, D_MODEL), DN_WIDTH ** -0.5),
        'w_branch_hg': nrm(ks[10], (DEPTH, HG_VAL_WIDTH, D_MODEL), HG_VAL_WIDTH ** -0.5),
        'w_out': nrm(ks[11], (DEPTH, D_MODEL, D_MODEL), D_MODEL ** -0.5),
        'ffn_norm': 1.0 + nrm(ks[12], (DEPTH, D_MODEL), 0.02),
        'w_up': nrm(ks[13], (DEPTH, D_MODEL, 2 * D_FF), D_MODEL ** -0.5),
        'ffn_conv': nrm(ks[14], (DEPTH, FFN_CONV, D_FF), FFN_CONV ** -0.5),
        'ffn_conv_bias': nrm(ks[15], (DEPTH, D_FF), 0.02),
        'w_down': nrm(ks[16], (DEPTH, D_FF, D_MODEL), D_FF ** -0.5),
        'final_norm': 1.0 + nrm(ks[17], (D_MODEL,), 0.02),
    }


def reference(x, mix_norm, w_in, dn_conv, dn_a_log, dn_dt_bias, dn_norm, hg_lb_logits, hg_norm,
              w_branch_dn, w_branch_hg, w_out, ffn_norm, w_up, ffn_conv, ffn_conv_bias, w_down, final_norm):
    sm = jax.nn.softmax(hg_lb_logits.astype(jnp.float32), axis=0)
    lb_all = jnp.clip(jnp.cumsum(sm, axis=0) - sm[0:1], 0.0, 1.0)
    for l in range(DEPTH):
        h = _rmsnorm(x, mix_norm[l])
        x = x + _mixer(h, w_in[l], dn_conv[l], dn_a_log[l], dn_dt_bias[l], dn_norm[l], lb_all[l],
                       hg_norm[l], w_branch_dn[l], w_branch_hg[l], w_out[l])
        h = _rmsnorm(x, ffn_norm[l])
        x = x + _conv_ffn(h, w_up[l], ffn_conv[l], ffn_conv_bias[l], w_down[l])
    return _rmsnorm(x, final_norm)
```

```python
import functools

import jax
import jax.numpy as jnp
from jax import lax
from jax.experimental import pallas as pl
from jax.experimental.pallas import tpu as pltpu

F32 = jnp.float32
BF16 = jnp.bfloat16

LANES = 128
D_MODEL = 1024
N_HEADS = 4
HEAD_DIM = 128
MIX_WIDTH = N_HEADS * HEAD_DIM
DN_CONV = 5
D_FF = 2816
EPS = 1e-6
CHUNK = 64
SUB = 16
ACT_DTYPE = BF16
VMEM_LIMIT = 56 * 1024 * 1024

N_MAIN = 2 * D_MODEL + 4 * MIX_WIDTH + 5 * MIX_WIDTH
T_GATE_DN, T_GATE_HG = 0, 1
T_Q, T_K, T_V, T_Z = 16, 20, 24, 28
T_HQ, T_HFF, T_HFB, T_HI, T_HGATE = 32, 36, 40, 44, 48
N_SMALL = 16

_C_Z_END = 4 * MIX_WIDTH
_C_HQ = _C_Z_END + N_SMALL
_C_GATE = _C_HQ + 5 * MIX_WIDTH


def _sigmoid(x):
    return 1.0 / (1.0 + jnp.exp(-x))


def _softplus(x):
    return jnp.maximum(x, 0.0) + jnp.log1p(jnp.exp(-jnp.abs(x)))


def _rms(x, w):
    return x * lax.rsqrt(jnp.mean(x * x, axis=-1, keepdims=True) + EPS) * w


def _dot(a, b):
    return jnp.dot(a, b, preferred_element_type=F32)


def _dot_nt(a, b):
    return lax.dot_general(a, b, (((1,), (1,)), ((), ())), preferred_element_type=F32)


def _dot_tn(a, b):
    return lax.dot_general(a, b, (((0,), (0,)), ((), ())), preferred_element_type=F32)


def _split3(x):
    hi = x.astype(BF16)
    r = x - hi.astype(F32)
    mid = r.astype(BF16)
    lo = (r - mid.astype(F32)).astype(BF16)
    return hi, mid, lo


def _sel_left(m16, x):
    hi, mid, lo = _split3(x)
    return _dot(m16, hi) + _dot(m16, mid) + _dot(m16, lo)


def _sel_right(x, m16):
    hi, mid, lo = _split3(x)
    return _dot(hi, m16) + _dot(mid, m16) + _dot(lo, m16)


def _tri(n, lower, strict=False):
    r = lax.broadcasted_iota(jnp.int32, (n, n), 0)
    c = lax.broadcasted_iota(jnp.int32, (n, n), 1)
    if lower:
        return (r > c) if strict else (r >= c)
    return (r < c) if strict else (r <= c)


def _inproj_kernel(x_ref, nw_ref, w_ref, ws_ref, wst_ref, p_ref, s_ref, st_ref, h_ref):
    @pl.when(pl.program_id(1) == 0)
    def _():
        hb = _rms(x_ref[...], nw_ref[...]).astype(BF16)
        h_ref[...] = hb
        s_ref[...] = _dot(hb, ws_ref[...])
        st_ref[...] = _dot_nt(wst_ref[...], hb)

    p_ref[...] = _dot(h_ref[...], w_ref[...]).astype(p_ref.dtype)


def _inproj(x2d, norm_w, w_main, w_small, w_small_t):
    t = x2d.shape[0]
    tm = min(1024, t)
    tn = N_MAIN // 4
    return pl.pallas_call(
        _inproj_kernel,
        grid=(t // tm, N_MAIN // tn),
        in_specs=[
            pl.BlockSpec((tm, D_MODEL), lambda i, j: (i, 0)),
            pl.BlockSpec((1, D_MODEL), lambda i, j: (0, 0)),
            pl.BlockSpec((D_MODEL, tn), lambda i, j: (0, j)),
            pl.BlockSpec((D_MODEL, LANES), lambda i, j: (0, 0)),
            pl.BlockSpec((N_SMALL, D_MODEL), lambda i, j: (0, 0)),
        ],
        out_specs=[
            pl.BlockSpec((tm, tn), lambda i, j: (i, j)),
            pl.BlockSpec((tm, LANES), lambda i, j: (i, 0)),
            pl.BlockSpec((N_SMALL, tm), lambda i, j: (0, i)),
        ],
        out_shape=[
            jax.ShapeDtypeStruct((t, N_MAIN), ACT_DTYPE),
            jax.ShapeDtypeStruct((t, LANES), F32),
            jax.ShapeDtypeStruct((N_SMALL, t), F32),
        ],
        scratch_shapes=[pltpu.VMEM((tm, D_MODEL), BF16)],
        compiler_params=pltpu.CompilerParams(
            dimension_semantics=("arbitrary", "arbitrary"), vmem_limit_bytes=VMEM_LIMIT),
        name="inproj",
    )(x2d, norm_w, w_main, w_small, w_small_t)


def _neumann_inverse(a):
    n = a.shape[0]
    eye = (lax.broadcasted_iota(jnp.int32, (n, n), 0) == lax.broadcasted_iota(jnp.int32, (n, n), 1)).astype(F32)
    p = eye - a
    x = a
    steps = n.bit_length() - 1
    for _ in range(steps - 1):
        x = jnp.dot(x, x, preferred_element_type=F32, precision=lax.Precision.HIGHEST)
        p = p + jnp.dot(p, x, preferred_element_type=F32, precision=lax.Precision.HIGHEST)
    return p


def _dn_kernel(alog_ref, dtb_ref, q_ref, k_ref, v_ref, z_ref, small_ref, smallt_ref,
               cq_ref, ck_ref, cv_ref, nw_ref, o_ref,
               pad_scr, qs, ks, vs, bcol, gcol, grow, st_scr, of_scr):
    h = pl.program_id(1)
    s_len = q_ref.shape[0]
    nc = s_len // CHUNK
    lane = lax.broadcasted_iota(jnp.int32, (1, LANES), 1)

    pad_scr[0:8, :] = jnp.zeros((8, LANES), F32)
    pad_scr[8 + s_len:16 + s_len, :] = jnp.zeros((8, LANES), F32)

    def conv_silu(x_ref, cw_ref):
        pad_scr[8:8 + s_len, :] = x_ref[...].astype(F32)
        y = cw_ref[0:1, :] * pad_scr[6:6 + s_len, :]
        for j in range(1, DN_CONV):
            y = y + cw_ref[j:j + 1, :] * pad_scr[6 + j:6 + j + s_len, :]
        return y * _sigmoid(y)

    def l2n(y):
        return y * lax.rsqrt(jnp.sum(y * y, axis=-1, keepdims=True) + EPS)

    qs[...] = l2n(conv_silu(q_ref, cq_ref)) * (HEAD_DIM ** -0.5)
    ks[...] = l2n(conv_silu(k_ref, ck_ref))
    vs[...] = conv_silu(v_ref, cv_ref)

    alv = jnp.zeros((1, LANES), F32)
    dtv = jnp.zeros((1, LANES), F32)
    for d in range(2):
        for hh in range(N_HEADS):
            alv = jnp.where(lane == 8 + d * N_HEADS + hh, alog_ref[d, hh], alv)
            dtv = jnp.where(lane == 8 + d * N_HEADS + hh, dtb_ref[d, hh], dtv)
    sm = small_ref[...]
    beta_all = _sigmoid(sm)
    g_all = -jnp.exp(alv) * _softplus(sm + dtv)

    def pick(arr, idx):
        col = jnp.sum(jnp.where(lane == idx, arr, 0.0), axis=-1, keepdims=True)
        return jnp.broadcast_to(col, (s_len, LANES))

    tri_lo = _tri(CHUNK, True).astype(BF16)
    tri_up = _tri(CHUNK, False).astype(BF16)
    for d in range(2):
        bcol[d] = pick(beta_all, d * N_HEADS + h)
        gcol[d] = pick(g_all, 8 + d * N_HEADS + h)
        m16 = tri_lo if d == 0 else tri_up

        def cumulate(c, carry, d=d, m16=m16):
            rows = pl.ds(pl.multiple_of(c * CHUNK, CHUNK), CHUNK)
            gcol[d, rows, :] = _sel_left(m16, gcol[d, rows, :])
            return carry

        lax.fori_loop(0, nc, cumulate, 0)

    for d in range(2):
        al = jnp.full((1, CHUNK), alog_ref[d, h], F32)
        dt = jnp.full((1, CHUNK), dtb_ref[d, h], F32)
        g_r = -jnp.exp(al) * _softplus(smallt_ref[0, 8 + d * N_HEADS + h] + dt)
        m16 = tri_up if d == 0 else tri_lo
        grow[d] = _sel_right(g_r, m16)

    st_scr[...] = jnp.zeros(st_scr.shape, F32)
    incl = (_tri(CHUNK, True), _tri(CHUNK, False))
    strict = (_tri(CHUNK, True, True), _tri(CHUNK, False, True))

    def chunk(d, c):
        rows = pl.ds(pl.multiple_of(c * CHUNK, CHUNK), CHUNK)
        q = qs[rows, :]
        k = ks[rows, :]
        v = vs[rows, :]
        gc_c = gcol[d, rows, :]
        be_c = bcol[d, rows, :]
        gc_r = grow[d, pl.ds(c, 1), :]
        diff = gc_c[:, :CHUNK] - gc_r
        decay = jnp.where(incl[d], jnp.exp(jnp.where(incl[d], diff, 0.0)), 0.0)
        k16 = k.astype(BF16)
        kk = _dot_nt(k16, k16)
        a = jnp.where(strict[d], kk * be_c[:, :CHUNK] * decay, 0.0)
        t_inv = _neumann_inverse(a)
        qk = _dot_nt(q.astype(BF16), k16) * decay
        e_gc = jnp.exp(gc_c)
        rhs = jnp.concatenate([v * be_c, k * (be_c * e_gc)], axis=1).astype(BF16)
        uw = _dot(t_inv.astype(BF16), rhs)
        state = st_scr[d]
        lhs = jnp.concatenate([uw[:, HEAD_DIM:], q * e_gc], axis=0).astype(BF16)
        ws_qs = _dot(lhs, state.astype(BF16))
        v_new = uw[:, :HEAD_DIM] - ws_qs[:CHUNK]
        v16 = v_new.astype(BF16)
        o = ws_qs[CHUNK:] + _dot(qk.astype(BF16), v16)
        last = CHUNK - 1 if d == 0 else 0
        gc_last = gc_c[last:last + 1, :]
        kd = (k * jnp.exp(gc_last - gc_c)).astype(BF16)
        st_scr[d] = state * jnp.exp(gc_last) + _dot_tn(kd, v16)
        return rows, o

    half = nc // 2

    def step_first(i, carry):
        rows, o = chunk(0, i)
        of_scr[rows, :] = o
        rows, o = chunk(1, nc - 1 - i)
        of_scr[rows, :] = o
        return carry

    def step_second(i, carry):
        rows, o = chunk(0, i)
        of_scr[rows, :] += o
        rows, o = chunk(1, nc - 1 - i)
        of_scr[rows, :] += o
        return carry

    lax.fori_loop(0, half, step_first, 0)
    lax.fori_loop(half, nc, step_second, 0)

    o = of_scr[...]
    y = o * lax.rsqrt(jnp.mean(o * o, axis=-1, keepdims=True) + EPS) * nw_ref[...]
    zz = z_ref[...].astype(F32)
    o_ref[...] = (y * (zz * _sigmoid(zz))).astype(o_ref.dtype)


def _deltanet(p_main, small, small_t, conv_w, a_log, dt_bias, norm_w, batch, s_len):
    nc = s_len // CHUNK
    assert nc % 2 == 0
    tok = lambda tile: pl.BlockSpec((s_len, LANES), lambda b, h, tile=tile: (b, tile + h))
    cw = lambda tile: pl.BlockSpec((DN_CONV, LANES), lambda b, h, tile=tile: (0, tile + h))
    smem = pl.BlockSpec(memory_space=pltpu.SMEM)
    return pl.pallas_call(
        _dn_kernel,
        grid=(batch, N_HEADS),
        in_specs=[
            smem, smem,
            tok(T_Q), tok(T_K), tok(T_V), tok(T_Z),
            pl.BlockSpec((s_len, LANES), lambda b, h: (b, 0)),
            pl.BlockSpec((1, N_SMALL, nc, CHUNK), lambda b, h: (b, 0, 0, 0)),
            cw(0), cw(N_HEADS), cw(2 * N_HEADS),
            pl.BlockSpec((1, LANES), lambda b, h: (0, 0)),
        ],
        out_specs=pl.BlockSpec((s_len, LANES), lambda b, h: (b, h)),
        out_shape=jax.ShapeDtypeStruct((batch * s_len, MIX_WIDTH), ACT_DTYPE),
        scratch_shapes=[
            pltpu.VMEM((s_len + 16, LANES), F32),
            pltpu.VMEM((s_len, LANES), F32),
            pltpu.VMEM((s_len, LANES), F32),
            pltpu.VMEM((s_len, LANES), F32),
            pltpu.VMEM((2, s_len, LANES), F32),
            pltpu.VMEM((2, s_len, LANES), F32),
            pltpu.VMEM((2, nc, CHUNK), F32),
            pltpu.VMEM((2, HEAD_DIM, HEAD_DIM), F32),
            pltpu.VMEM((s_len, LANES), F32),
        ],
        compiler_params=pltpu.CompilerParams(
            dimension_semantics=("arbitrary", "arbitrary"), vmem_limit_bytes=VMEM_LIMIT),
        name="deltanet",
    )(a_log, dt_bias, p_main, p_main, p_main, p_main, small, small_t,
      conv_w, conv_w, conv_w, norm_w)


def _hg_kernel(layer, hq_ref, hff_ref, hfb_ref, hi_ref, hg_ref, lbl_ref, nw_ref, o_ref,
               qs, iv, lf, kk, st_scr, of_scr):
    s_len = hq_ref.shape[0]
    nc = s_len // CHUNK
    nsub = CHUNK // SUB

    lg = lbl_ref[...].astype(F32)
    e = jnp.exp(lg - jnp.max(lg, axis=0, keepdims=True))
    sm = e / jnp.sum(e, axis=0, keepdims=True)
    lb = jnp.clip(jnp.sum(sm[:layer + 1], axis=0) - sm[0], 0.0, 1.0)

    x = hq_ref[...].astype(F32)
    qs[...] = (x * _sigmoid(x)) * (HEAD_DIM ** -0.5)
    iv[...] = hi_ref[...].astype(F32)
    for d, ref in enumerate((hff_ref, hfb_ref)):
        fr = ref[...].astype(F32)
        lbd = lb[d:d + 1, :]
        log_sig = jnp.minimum(fr, 0.0) - jnp.log1p(jnp.exp(-jnp.abs(fr)))
        lf[d] = log_sig + jnp.log1p(lbd * jnp.exp(-fr))
        kk[d] = (1.0 - lbd) * _sigmoid(-fr)

    st_scr[...] = jnp.zeros(st_scr.shape, F32)
    tri16 = (_tri(CHUNK, True).astype(BF16), _tri(CHUNK, False).astype(BF16))
    r16 = lax.broadcasted_iota(jnp.int32, (SUB, 1), 0)

    def chunk(d, c):
        rows = pl.ds(pl.multiple_of(c * CHUNK, CHUNK), CHUNK)
        q = qs[rows, :]
        k = kk[d, rows, :]
        i_c = iv[rows, :]
        b = _sel_left(tri16[d], lf[d, rows, :])
        last = CHUNK - 1 if d == 0 else 0
        b_last = b[last:last + 1, :]
        state_t = st_scr[d]
        i16 = i_c.astype(BF16)
        o = _dot_nt((q * jnp.exp(b)).astype(BF16), state_t.astype(BF16))
        kt = (k * jnp.exp(b_last - b)).astype(BF16)
        st_scr[d] = state_t * jnp.exp(b_last) + _dot_tn(i16, kt)

        blocks = []
        for blk in range(nsub):
            r0 = blk * SUB
            b_i = b[r0:r0 + SUB, :]
            q_i = q[r0:r0 + SUB, :]
            k_i = k[r0:r0 + SUB, :]
            i_i = i_c[r0:r0 + SUB, :]
            acc = o[r0:r0 + SUB, :]
            if d == 0 and blk > 0:
                ref_b = b_i[0:1, :]
                prev = slice(0, r0)
            elif d == 1 and blk < nsub - 1:
                ref_b = b_i[SUB - 1:SUB, :]
                prev = slice(r0 + SUB, CHUNK)
            else:
                prev = None
            if prev is not None:
                qt = (q_i * jnp.exp(b_i - ref_b)).astype(BF16)
                kp = (k[prev, :] * jnp.exp(ref_b - b[prev, :])).astype(BF16)
                att = _dot_nt(qt, kp)
                acc = acc + _dot(att.astype(BF16), i16[prev, :])
            for s in range(SUB):
                ok = (r16 >= s) if d == 0 else (r16 <= s)
                ex = jnp.exp(jnp.where(ok, b_i - b_i[s:s + 1, :], 0.0))
                col = jnp.sum(q_i * k_i[s:s + 1, :] * ex, axis=-1, keepdims=True)
                acc = acc + jnp.where(ok, col, 0.0) * i_i[s:s + 1, :]
            blocks.append(acc)
        return rows, jnp.concatenate(blocks, axis=0)

    half = nc // 2

    def step_first(i, carry):
        rows, o = chunk(0, i)
        of_scr[rows, :] = o
        rows, o = chunk(1, nc - 1 - i)
        of_scr[rows, :] = o
        return carry

    def step_second(i, carry):
        rows, o = chunk(0, i)
        of_scr[rows, :] += o
        rows, o = chunk(1, nc - 1 - i)
        of_scr[rows, :] += o
        return carry

    lax.fori_loop(0, half, step_first, 0)
    lax.fori_loop(half, nc, step_second, 0)

    o = of_scr[...]
    y = o * lax.rsqrt(jnp.mean(o * o, axis=-1, keepdims=True) + EPS) * nw_ref[...]
    zz = hg_ref[...].astype(F32)
    o_ref[...] = (y * (zz * _sigmoid(zz))).astype(o_ref.dtype)


def _hgrn2(p_main, lb_logits, norm_w, layer, batch, s_len):
    nc = s_len // CHUNK
    assert nc % 2 == 0
    depth = lb_logits.shape[0]
    tok = lambda tile: pl.BlockSpec((s_len, LANES), lambda b, h, tile=tile: (b, tile + h))
    return pl.pallas_call(
        functools.partial(_hg_kernel, layer),
        grid=(batch, N_HEADS),
        in_specs=[
            tok(T_HQ), tok(T_HFF), tok(T_HFB), tok(T_HI), tok(T_HGATE),
            pl.BlockSpec((depth, 2, LANES), lambda b, h: (0, 0, h)),
            pl.BlockSpec((1, LANES), lambda b, h: (0, 0)),
        ],
        out_specs=pl.BlockSpec((s_len, LANES), lambda b, h: (b, h)),
        out_shape=jax.ShapeDtypeStruct((batch * s_len, MIX_WIDTH), ACT_DTYPE),
        scratch_shapes=[
            pltpu.VMEM((s_len, LANES), F32),
            pltpu.VMEM((s_len, LANES), F32),
            pltpu.VMEM((2, s_len, LANES), F32),
            pltpu.VMEM((2, s_len, LANES), F32),
            pltpu.VMEM((2, HEAD_DIM, HEAD_DIM), F32),
            pltpu.VMEM((s_len, LANES), F32),
        ],
        compiler_params=pltpu.CompilerParams(
            dimension_semantics=("arbitrary", "arbitrary"), vmem_limit_bytes=VMEM_LIMIT),
        name="hgrn2",
    )(p_main, p_main, p_main, p_main, p_main, lb_logits, norm_w)


def _merge_kernel(x_ref, odn_ref, ohg_ref, gdn_ref, ghg_ref, wdn_ref, whg_ref, wo_ref, o_ref):
    a = _dot(odn_ref[...].astype(BF16), wdn_ref[...])
    b = _dot(ohg_ref[...].astype(BF16), whg_ref[...])
    merged = _sigmoid(gdn_ref[...].astype(F32)) * a + _sigmoid(ghg_ref[...].astype(F32)) * b
    o_ref[...] = x_ref[...] + _dot(merged.astype(BF16), wo_ref[...])


def _merge_out(x2d, o_dn, o_hg, p_main, w_dn, w_hg, w_out):
    t = x2d.shape[0]
    tm = min(1024, t)
    const = lambda shape: pl.BlockSpec(shape, lambda i: (0, 0))
    return pl.pallas_call(
        _merge_kernel,
        grid=(t // tm,),
        in_specs=[
            pl.BlockSpec((tm, D_MODEL), lambda i: (i, 0)),
            pl.BlockSpec((tm, MIX_WIDTH), lambda i: (i, 0)),
            pl.BlockSpec((tm, MIX_WIDTH), lambda i: (i, 0)),
            pl.BlockSpec((tm, D_MODEL), lambda i: (i, T_GATE_DN)),
            pl.BlockSpec((tm, D_MODEL), lambda i: (i, T_GATE_HG)),
            const((MIX_WIDTH, D_MODEL)), const((MIX_WIDTH, D_MODEL)), const((D_MODEL, D_MODEL)),
        ],
        out_specs=pl.BlockSpec((tm, D_MODEL), lambda i: (i, 0)),
        out_shape=jax.ShapeDtypeStruct((t, D_MODEL), F32),
        compiler_params=pltpu.CompilerParams(
            dimension_semantics=("arbitrary",), vmem_limit_bytes=VMEM_LIMIT),
        name="merge_out",
    )(x2d, o_dn, o_hg, p_main, p_main, w_dn, w_hg, w_out)


FF_COLS = 256


def _ffn_kernel(final, xm_ref, xp_ref, xn_ref, nw_ref, wg_ref, wu_ref, cw_ref, cb_ref, wd_ref, fn_ref,
                o_ref, g_scr):
    i = pl.program_id(1)
    last = pl.num_programs(1) - 1
    ts = xm_ref.shape[1]
    x = xm_ref[0]
    nw = nw_ref[...]
    h = _rms(x, nw).astype(BF16)
    hh = _rms(jnp.concatenate([xp_ref[0], xn_ref[0]], axis=0), nw).astype(BF16)
    acc = x
    for c in range(D_FF // FF_COLS):
        cols = slice(c * FF_COLS, (c + 1) * FF_COLS)
        wg = wg_ref[:, cols]
        g = _dot(h, wg)
        gh = _dot(hh, wg)
        g_scr[8:8 + ts, :] = g
        g_scr[7:8, :] = jnp.where(i > 0, gh[7:8, :], 0.0)
        g_scr[8 + ts:9 + ts, :] = jnp.where(i < last, gh[8:9, :], 0.0)
        gc = (cw_ref[0:1, cols] * g_scr[7:7 + ts, :] + cw_ref[1:2, cols] * g
              + cw_ref[2:3, cols] * g_scr[9:9 + ts, :] + cb_ref[:, cols])
        u = _dot(h, wu_ref[:, cols])
        act = (gc * _sigmoid(gc)) * u
        acc = acc + _dot(act.astype(BF16), wd_ref[cols, :])
    if final:
        acc = _rms(acc, fn_ref[...])
    o_ref[0] = acc


def _ffn(x3d, norm_w, w_gate, w_upp, conv_w, conv_b, w_down, final_w, final):
    batch, s_len, _ = x3d.shape
    ts = min(512, s_len)
    r8 = ts // 8
    n8 = s_len // 8
    const = lambda shape: pl.BlockSpec(shape, lambda b, i: (0, 0), pipeline_mode=pl.Buffered(1))
    return pl.pallas_call(
        functools.partial(_ffn_kernel, final),
        grid=(batch, s_len // ts),
        in_specs=[
            pl.BlockSpec((1, ts, D_MODEL), lambda b, i: (b, i, 0)),
            pl.BlockSpec((1, 8, D_MODEL), lambda b, i: (b, jnp.maximum(i * r8 - 1, 0), 0)),
            pl.BlockSpec((1, 8, D_MODEL), lambda b, i: (b, jnp.minimum((i + 1) * r8, n8 - 1), 0)),
            const((1, D_MODEL)),
            const((D_MODEL, D_FF)), const((D_MODEL, D_FF)),
            const((3, D_FF)), const((1, D_FF)),
            const((D_FF, D_MODEL)),
            const((1, D_MODEL)),
        ],
        out_specs=pl.BlockSpec((1, ts, D_MODEL), lambda b, i: (b, i, 0)),
        out_shape=jax.ShapeDtypeStruct(x3d.shape, F32),
        scratch_shapes=[pltpu.VMEM((ts + 16, FF_COLS), F32)],
        compiler_params=pltpu.CompilerParams(
            dimension_semantics=("arbitrary", "arbitrary"), vmem_limit_bytes=VMEM_LIMIT),
        name="ffn",
    )(x3d, x3d, x3d, norm_w, w_gate, w_upp, conv_w, conv_b, w_down, final_w)


def kernel(x, mix_norm, w_in, dn_conv, dn_a_log, dn_dt_bias, dn_norm, hg_lb_logits, hg_norm,
           w_branch_dn, w_branch_hg, w_out, ffn_norm, w_up, ffn_conv, ffn_conv_bias, w_down, final_norm):
    batch, s_len, _ = x.shape
    depth = mix_norm.shape[0]
    nc = s_len // CHUNK
    x2d = x.reshape(batch * s_len, D_MODEL)
    final_w = final_norm.reshape(1, D_MODEL)
    for l in range(depth):
        w = w_in[l]
        w_main = jnp.concatenate(
            [w[:, _C_GATE:], w[:, :_C_Z_END], w[:, _C_HQ:_C_GATE]], axis=1).astype(BF16)
        w_s = w[:, _C_Z_END:_C_HQ].astype(BF16)
        w_small = jnp.pad(w_s, ((0, 0), (0, LANES - N_SMALL)))
        p_main, small, small_t = _inproj(x2d, mix_norm[l].reshape(1, D_MODEL), w_main, w_small, w_s.T)
        small_t = small_t.reshape(N_SMALL, batch, nc, CHUNK).transpose(1, 0, 2, 3)
        o_dn = _deltanet(p_main, small, small_t, dn_conv[l], dn_a_log[l], dn_dt_bias[l],
                         dn_norm[l].reshape(1, HEAD_DIM), batch, s_len)
        o_hg = _hgrn2(p_main, hg_lb_logits, hg_norm[l].reshape(1, HEAD_DIM), l, batch, s_len)
        x2d = _merge_out(x2d, o_dn, o_hg, p_main, w_branch_dn[l].astype(BF16),
                         w_branch_hg[l].astype(BF16), w_out[l].astype(BF16))
        wu = w_up[l].astype(BF16)
        x3d = _ffn(x2d.reshape(batch, s_len, D_MODEL), ffn_norm[l].reshape(1, D_MODEL),
                   wu[:, :D_FF], wu[:, D_FF:], ffn_conv[l], ffn_conv_bias[l].reshape(1, D_FF),
                   w_down[l].astype(BF16), final_w, l == depth - 1)
        x2d = x3d.reshape(batch * s_len, D_MODEL)
    return x2d.reshape(batch, s_len, D_MODEL)
```

```python
import functools

import jax
import jax.numpy as jnp
from jax import lax
from jax.experimental import pallas as pl
from jax.experimental.pallas import tpu as pltpu

F32 = jnp.float32
BF16 = jnp.bfloat16

LANES = 128
D_MODEL = 1024
N_HEADS = 4
HEAD_DIM = 128
MIX_WIDTH = N_HEADS * HEAD_DIM
DN_CONV = 5
D_FF = 2816
EPS = 1e-6
CHUNK = 64
SUB = 16
CHUNK_SHIFT = CHUNK.bit_length() - 1
DN_SUPER = 256
ACT_DTYPE = BF16
VMEM_LIMIT = 56 * 1024 * 1024

N_MAIN = 2 * D_MODEL + 4 * MIX_WIDTH + 5 * MIX_WIDTH
T_GATE_DN, T_GATE_HG = 0, 1
T_Q, T_K, T_V, T_Z = 16, 20, 24, 28
T_HQ, T_HFF, T_HFB, T_HI, T_HGATE = 32, 36, 40, 44, 48
N_SMALL = 16

_C_Z_END = 4 * MIX_WIDTH
_C_HQ = _C_Z_END + N_SMALL
_C_GATE = _C_HQ + 5 * MIX_WIDTH


def _sigmoid(x):
    return 1.0 / (1.0 + jnp.exp(-x))


def _softplus(x):
    return jnp.maximum(x, 0.0) + jnp.log1p(jnp.exp(-jnp.abs(x)))


def _rms(x, w):
    return x * lax.rsqrt(jnp.mean(x * x, axis=-1, keepdims=True) + EPS) * w


def _dot(a, b):
    return jnp.dot(a, b, preferred_element_type=F32)


def _dot_nt(a, b):
    return lax.dot_general(a, b, (((1,), (1,)), ((), ())), preferred_element_type=F32)


def _dot_tn(a, b):
    return lax.dot_general(a, b, (((0,), (0,)), ((), ())), preferred_element_type=F32)


def _split3(x):
    hi = x.astype(BF16)
    r = x - hi.astype(F32)
    mid = r.astype(BF16)
    lo = (r - mid.astype(F32)).astype(BF16)
    return hi, mid, lo


def _sel_left(m16, x):
    hi, mid, lo = _split3(x)
    return _dot(m16, hi) + _dot(m16, mid) + _dot(m16, lo)


def _sel_right(x, m16):
    hi, mid, lo = _split3(x)
    return _dot(hi, m16) + _dot(mid, m16) + _dot(lo, m16)


def _tri(n, lower, strict=False):
    r = lax.broadcasted_iota(jnp.int32, (n, n), 0)
    c = lax.broadcasted_iota(jnp.int32, (n, n), 1)
    if lower:
        return (r > c) if strict else (r >= c)
    return (r < c) if strict else (r <= c)


def _inproj_kernel(x_ref, nw_ref, w_ref, ws_ref, wst_ref, p_ref, s_ref, st_ref, h_ref):
    @pl.when(pl.program_id(1) == 0)
    def _():
        hb = _rms(x_ref[...], nw_ref[...]).astype(BF16)
        h_ref[...] = hb
        s_ref[...] = _dot(hb, ws_ref[...])
        st_ref[...] = _dot_nt(wst_ref[...], hb)

    p_ref[...] = _dot(h_ref[...], w_ref[...]).astype(p_ref.dtype)


def _inproj(x2d, norm_w, w_main, w_small, w_small_t):
    t = x2d.shape[0]
    tm = min(1024, t)
    tn = N_MAIN // 4
    return pl.pallas_call(
        _inproj_kernel,
        grid=(t // tm, N_MAIN // tn),
        in_specs=[
            pl.BlockSpec((tm, D_MODEL), lambda i, j: (i, 0)),
            pl.BlockSpec((1, D_MODEL), lambda i, j: (0, 0)),
            pl.BlockSpec((D_MODEL, tn), lambda i, j: (0, j)),
            pl.BlockSpec((D_MODEL, LANES), lambda i, j: (0, 0)),
            pl.BlockSpec((N_SMALL, D_MODEL), lambda i, j: (0, 0)),
        ],
        out_specs=[
            pl.BlockSpec((tm, tn), lambda i, j: (i, j)),
            pl.BlockSpec((tm, LANES), lambda i, j: (i, 0)),
            pl.BlockSpec((N_SMALL, tm), lambda i, j: (0, i)),
        ],
        out_shape=[
            jax.ShapeDtypeStruct((t, N_MAIN), ACT_DTYPE),
            jax.ShapeDtypeStruct((t, LANES), F32),
            jax.ShapeDtypeStruct((N_SMALL, t), F32),
        ],
        scratch_shapes=[pltpu.VMEM((tm, D_MODEL), BF16)],
        compiler_params=pltpu.CompilerParams(
            dimension_semantics=("arbitrary", "arbitrary"), vmem_limit_bytes=VMEM_LIMIT),
        name="inproj",
    )(x2d, norm_w, w_main, w_small, w_small_t)


def _neumann_inverse(a, block):
    n = a.shape[0]
    eye = (lax.broadcasted_iota(jnp.int32, (n, n), 0) == lax.broadcasted_iota(jnp.int32, (n, n), 1)).astype(F32)
    p = eye - a
    x = a
    steps = block.bit_length() - 1
    for _ in range(steps - 1):
        x16 = x.astype(BF16)
        x = _dot(x16, x16)
        p = p + _dot(p.astype(BF16), x.astype(BF16))
    return p


def _dn_kernel(alog_ref, dtb_ref, q_ref, k_ref, v_ref, z_ref, small_ref, smallt_ref,
               cq_ref, ck_ref, cv_ref, nw_ref, o_ref,
               pad_scr, qs, ks, vs, ball, gcall, bcol, gcol, grow, st_scr, of_scr, mt_scr, nt_scr, qh_scr,
               gl_scr):
    h = pl.program_id(1)
    s_len = q_ref.shape[0]
    nc = s_len // CHUNK
    lane = lax.broadcasted_iota(jnp.int32, (1, LANES), 1)

    pad_scr[0:8, :] = jnp.zeros((8, LANES), F32)
    pad_scr[8 + s_len:16 + s_len, :] = jnp.zeros((8, LANES), F32)

    def conv_silu(x_ref, cw_ref):
        pad_scr[8:8 + s_len, :] = x_ref[...].astype(F32)
        y = cw_ref[0:1, :] * pad_scr[6:6 + s_len, :]
        for j in range(1, DN_CONV):
            y = y + cw_ref[j:j + 1, :] * pad_scr[6 + j:6 + j + s_len, :]
        return y * _sigmoid(y)

    def l2n(y):
        return y * lax.rsqrt(jnp.sum(y * y, axis=-1, keepdims=True) + EPS)

    qs[...] = l2n(conv_silu(q_ref, cq_ref)) * (HEAD_DIM ** -0.5)
    ks[...] = l2n(conv_silu(k_ref, ck_ref))
    vs[...] = conv_silu(v_ref, cv_ref)

    n_sup = s_len // DN_SUPER
    per_sup = DN_SUPER // CHUNK
    r_i = lax.broadcasted_iota(jnp.int32, (DN_SUPER, DN_SUPER), 0)
    c_i = lax.broadcasted_iota(jnp.int32, (DN_SUPER, DN_SUPER), 1)
    same = lax.shift_right_logical(r_i, CHUNK_SHIFT) == lax.shift_right_logical(c_i, CHUNK_SHIFT)
    incl = (same & (r_i >= c_i), same & (r_i <= c_i))
    strict = (same & (r_i > c_i), same & (r_i < c_i))

    @pl.when(h == 0)
    def _():
        alv = jnp.zeros((1, LANES), F32)
        dtv = jnp.zeros((1, LANES), F32)
        for d in range(2):
            for hh in range(N_HEADS):
                alv = jnp.where(lane == 8 + d * N_HEADS + hh, alog_ref[d, hh], alv)
                dtv = jnp.where(lane == 8 + d * N_HEADS + hh, dtb_ref[d, hh], dtv)
        sm = small_ref[...]
        ball[...] = _sigmoid(sm)
        gcall[...] = -jnp.exp(alv) * _softplus(sm + dtv)
        tri_lo = incl[0].astype(BF16)
        tri_up = incl[1].astype(BF16)
        fwd_lane = lane < 8 + N_HEADS

        def cumulate(g, carry):
            rows = pl.ds(pl.multiple_of(g * DN_SUPER, DN_SUPER), DN_SUPER)
            hi, mid, lo = _split3(gcall[rows, :])
            c_lo = _dot(tri_lo, hi) + _dot(tri_lo, mid) + _dot(tri_lo, lo)
            c_up = _dot(tri_up, hi) + _dot(tri_up, mid) + _dot(tri_up, lo)
            gcall[rows, :] = jnp.where(fwd_lane, c_lo, c_up)
            return carry

        lax.fori_loop(0, n_sup, cumulate, 0)

    def pick(ref, idx):
        col = jnp.sum(jnp.where(lane == idx, ref[...], 0.0), axis=-1, keepdims=True)
        return jnp.broadcast_to(col, (s_len, LANES))

    for d in range(2):
        bcol[d] = pick(ball, d * N_HEADS + h)
        gcol[d] = pick(gcall, 8 + d * N_HEADS + h)

    for d in range(2):
        al = jnp.full((1, DN_SUPER), alog_ref[d, h], F32)
        dt = jnp.full((1, DN_SUPER), dtb_ref[d, h], F32)
        g_r = -jnp.exp(al) * _softplus(smallt_ref[0, 8 + d * N_HEADS + h] + dt)
        grow[d] = _sel_right(g_r, incl[1 - d].astype(BF16))

    row_blk = lax.shift_right_logical(lax.broadcasted_iota(jnp.int32, (DN_SUPER, 1), 0), CHUNK_SHIFT)

    def prepare(g, carry):
        rows = pl.ds(pl.multiple_of(g * DN_SUPER, DN_SUPER), DN_SUPER)
        q = qs[rows, :]
        k = ks[rows, :]
        v = vs[rows, :]
        k16 = k.astype(BF16)
        kk = _dot_nt(k16, k16)
        qk_raw = _dot_nt(q.astype(BF16), k16)
        o_local = None
        for d in range(2):
            gc_c = gcol[d, rows, :]
            be_c = bcol[d, rows, :]
            gc_r = grow[d, pl.ds(g, 1), :]
            diff = jnp.concatenate([gc_c, gc_c], axis=1) - gc_r
            decay = jnp.where(incl[d], jnp.exp(jnp.where(incl[d], diff, 0.0)), 0.0)
            a = jnp.where(strict[d], kk * jnp.concatenate([be_c, be_c], axis=1) * decay, 0.0)
            t_inv = _neumann_inverse(a, CHUNK)
            e_gc = jnp.exp(gc_c)
            rhs = jnp.concatenate([v * be_c, k * (be_c * e_gc)], axis=1).astype(BF16)
            uw16 = _dot(t_inv.astype(BF16), rhs).astype(BF16)
            qk16 = (qk_raw * decay).astype(BF16)
            qu_qw = _dot(qk16, uw16)
            last = CHUNK - 1 if d == 0 else 0
            gc_last = [gc_c[j * CHUNK + last:j * CHUNK + last + 1, :] for j in range(per_sup)]
            gcl = jnp.concatenate([jnp.broadcast_to(x, (CHUNK, LANES)) for x in gc_last], axis=0)
            kd = k * jnp.exp(gcl - gc_c)
            kd_bd = jnp.concatenate(
                [jnp.where(row_blk == j, kd, 0.0) for j in range(per_sup)], axis=1).astype(BF16)
            mn = _dot_tn(uw16, kd_bd)
            for j in range(per_sup):
                c = g * per_sup + j
                cols = slice(j * HEAD_DIM, (j + 1) * HEAD_DIM)
                nt_scr[d, c] = mn[:HEAD_DIM, cols]
                mt_scr[d, c] = (-mn[HEAD_DIM:, cols]).astype(BF16)
                gl_scr[d, c] = jnp.exp(gc_last[j])
            qh_scr[d, rows, :] = (q * e_gc - qu_qw[:, HEAD_DIM:]).astype(BF16)
            o_local = qu_qw[:, :HEAD_DIM] if d == 0 else o_local + qu_qw[:, :HEAD_DIM]
        of_scr[rows, :] = o_local
        return carry

    lax.fori_loop(0, n_sup, prepare, 0)

    st_scr[...] = jnp.zeros(st_scr.shape, F32)

    def scan_step(i, carry):
        for d in range(2):
            c = i if d == 0 else nc - 1 - i
            rows = pl.ds(pl.multiple_of(c * CHUNK, CHUNK), CHUNK)
            st = st_scr[d]
            st16 = st.astype(BF16)
            of_scr[rows, :] += _dot_nt(qh_scr[d, rows, :], st16)
            st_scr[d] = st * gl_scr[d, c] + _dot(st16, mt_scr[d, c]) + nt_scr[d, c]
        return carry

    lax.fori_loop(0, nc, scan_step, 0)

    o = of_scr[...]
    y = o * lax.rsqrt(jnp.mean(o * o, axis=-1, keepdims=True) + EPS) * nw_ref[...]
    zz = z_ref[...].astype(F32)
    o_ref[...] = (y * (zz * _sigmoid(zz))).astype(o_ref.dtype)


def _deltanet(p_main, small, small_t, conv_w, a_log, dt_bias, norm_w, batch, s_len):
    nc = s_len // CHUNK
    n_sup = s_len // DN_SUPER
    assert s_len % DN_SUPER == 0
    tok = lambda tile: pl.BlockSpec((s_len, LANES), lambda b, h, tile=tile: (b, tile + h))
    cw = lambda tile: pl.BlockSpec((DN_CONV, LANES), lambda b, h, tile=tile: (0, tile + h))
    smem = pl.BlockSpec(memory_space=pltpu.SMEM)
    return pl.pallas_call(
        _dn_kernel,
        grid=(batch, N_HEADS),
        in_specs=[
            smem, smem,
            tok(T_Q), tok(T_K), tok(T_V), tok(T_Z),
            pl.BlockSpec((s_len, LANES), lambda b, h: (b, 0)),
            pl.BlockSpec((1, N_SMALL, n_sup, DN_SUPER), lambda b, h: (b, 0, 0, 0)),
            cw(0), cw(N_HEADS), cw(2 * N_HEADS),
            pl.BlockSpec((1, LANES), lambda b, h: (0, 0)),
        ],
        out_specs=pl.BlockSpec((s_len, LANES), lambda b, h: (b, h)),
        out_shape=jax.ShapeDtypeStruct((batch * s_len, MIX_WIDTH), ACT_DTYPE),
        scratch_shapes=[
            pltpu.VMEM((s_len + 16, LANES), F32),
            pltpu.VMEM((s_len, LANES), F32),
            pltpu.VMEM((s_len, LANES), F32),
            pltpu.VMEM((s_len, LANES), F32),
            pltpu.VMEM((s_len, LANES), F32),
            pltpu.VMEM((s_len, LANES), F32),
            pltpu.VMEM((2, s_len, LANES), F32),
            pltpu.VMEM((2, s_len, LANES), F32),
            pltpu.VMEM((2, n_sup, DN_SUPER), F32),
            pltpu.VMEM((2, HEAD_DIM, HEAD_DIM), F32),
            pltpu.VMEM((s_len, LANES), F32),
            pltpu.VMEM((2, nc, HEAD_DIM, HEAD_DIM), BF16),
            pltpu.VMEM((2, nc, HEAD_DIM, HEAD_DIM), F32),
            pltpu.VMEM((2, s_len, LANES), BF16),
            pltpu.VMEM((2, nc, 1, LANES), F32),
        ],
        compiler_params=pltpu.CompilerParams(
            dimension_semantics=("arbitrary", "arbitrary"), vmem_limit_bytes=VMEM_LIMIT),
        name="deltanet",
    )(a_log, dt_bias, p_main, p_main, p_main, p_main, small, small_t,
      conv_w, conv_w, conv_w, norm_w)


def _hg_kernel(layer, hq_ref, hff_ref, hfb_ref, hi_ref, hg_ref, lbl_ref, nw_ref, o_ref,
               qs, iv, lf, kk, st_scr, of_scr):
    s_len = hq_ref.shape[0]
    nc = s_len // CHUNK
    nsub = CHUNK // SUB

    lg = lbl_ref[...].astype(F32)
    e = jnp.exp(lg - jnp.max(lg, axis=0, keepdims=True))
    sm = e / jnp.sum(e, axis=0, keepdims=True)
    lb = jnp.clip(jnp.sum(sm[:layer + 1], axis=0) - sm[0], 0.0, 1.0)

    x = hq_ref[...].astype(F32)
    qs[...] = (x * _sigmoid(x)) * (HEAD_DIM ** -0.5)
    iv[...] = hi_ref[...].astype(F32)
    for d, ref in enumerate((hff_ref, hfb_ref)):
        fr = ref[...].astype(F32)
        lbd = lb[d:d + 1, :]
        log_sig = jnp.minimum(fr, 0.0) - jnp.log1p(jnp.exp(-jnp.abs(fr)))
        lf[d] = log_sig + jnp.log1p(lbd * jnp.exp(-fr))
        kk[d] = (1.0 - lbd) * _sigmoid(-fr)

    st_scr[...] = jnp.zeros(st_scr.shape, F32)
    tri16 = (_tri(CHUNK, True).astype(BF16), _tri(CHUNK, False).astype(BF16))
    r16 = lax.broadcasted_iota(jnp.int32, (SUB, 1), 0)

    def chunk(d, c):
        rows = pl.ds(pl.multiple_of(c * CHUNK, CHUNK), CHUNK)
        q = qs[rows, :]
        k = kk[d, rows, :]
        i_c = iv[rows, :]
        b = _sel_left(tri16[d], lf[d, rows, :])
        last = CHUNK - 1 if d == 0 else 0
        b_last = b[last:last + 1, :]
        state_t = st_scr[d]
        i16 = i_c.astype(BF16)
        o = _dot_nt((q * jnp.exp(b)).astype(BF16), state_t.astype(BF16))
        kt = (k * jnp.exp(b_last - b)).astype(BF16)
        st_scr[d] = state_t * jnp.exp(b_last) + _dot_tn(i16, kt)

        blocks = []
        for blk in range(nsub):
            r0 = blk * SUB
            b_i = b[r0:r0 + SUB, :]
            q_i = q[r0:r0 + SUB, :]
            k_i = k[r0:r0 + SUB, :]
            i_i = i_c[r0:r0 + SUB, :]
            acc = o[r0:r0 + SUB, :]
            if d == 0 and blk > 0:
                ref_b = b_i[0:1, :]
                prev = slice(0, r0)
            elif d == 1 and blk < nsub - 1:
                ref_b = b_i[SUB - 1:SUB, :]
                prev = slice(r0 + SUB, CHUNK)
            else:
                prev = None
            if prev is not None:
                qt = (q_i * jnp.exp(b_i - ref_b)).astype(BF16)
                kp = (k[prev, :] * jnp.exp(ref_b - b[prev, :])).astype(BF16)
                att = _dot_nt(qt, kp)
                acc = acc + _dot(att.astype(BF16), i16[prev, :])
            for s in range(SUB):
                ok = (r16 >= s) if d == 0 else (r16 <= s)
                ex = jnp.exp(jnp.where(ok, b_i - b_i[s:s + 1, :], 0.0))
                col = jnp.sum(q_i * k_i[s:s + 1, :] * ex, axis=-1, keepdims=True)
                acc = acc + jnp.where(ok, col, 0.0) * i_i[s:s + 1, :]
            blocks.append(acc)
        return rows, jnp.concatenate(blocks, axis=0)

    half = nc // 2

    def step_first(i, carry):
        rows, o = chunk(0, i)
        of_scr[rows, :] = o
        rows, o = chunk(1, nc - 1 - i)
        of_scr[rows, :] = o
        return carry

    def step_second(i, carry):
        rows, o = chunk(0, i)
        of_scr[rows, :] += o
        rows, o = chunk(1, nc - 1 - i)
        of_scr[rows, :] += o
        return carry

    lax.fori_loop(0, half, step_first, 0)
    lax.fori_loop(half, nc, step_second, 0)

    o = of_scr[...]
    y = o * lax.rsqrt(jnp.mean(o * o, axis=-1, keepdims=True) + EPS) * nw_ref[...]
    zz = hg_ref[...].astype(F32)
    o_ref[...] = (y * (zz * _sigmoid(zz))).astype(o_ref.dtype)


def _hgrn2(p_main, lb_logits, norm_w, layer, batch, s_len):
    nc = s_len // CHUNK
    assert nc % 2 == 0
    depth = lb_logits.shape[0]
    tok = lambda tile: pl.BlockSpec((s_len, LANES), lambda b, h, tile=tile: (b, tile + h))
    return pl.pallas_call(
        functools.partial(_hg_kernel, layer),
        grid=(batch, N_HEADS),
        in_specs=[
            tok(T_HQ), tok(T_HFF), tok(T_HFB), tok(T_HI), tok(T_HGATE),
            pl.BlockSpec((depth, 2, LANES), lambda b, h: (0, 0, h)),
            pl.BlockSpec((1, LANES), lambda b, h: (0, 0)),
        ],
        out_specs=pl.BlockSpec((s_len, LANES), lambda b, h: (b, h)),
        out_shape=jax.ShapeDtypeStruct((batch * s_len, MIX_WIDTH), ACT_DTYPE),
        scratch_shapes=[
            pltpu.VMEM((s_len, LANES), F32),
            pltpu.VMEM((s_len, LANES), F32),
            pltpu.VMEM((2, s_len, LANES), F32),
            pltpu.VMEM((2, s_len, LANES), F32),
            pltpu.VMEM((2, HEAD_DIM, HEAD_DIM), F32),
            pltpu.VMEM((s_len, LANES), F32),
        ],
        compiler_params=pltpu.CompilerParams(
            dimension_semantics=("arbitrary", "arbitrary"), vmem_limit_bytes=VMEM_LIMIT),
        name="hgrn2",
    )(p_main, p_main, p_main, p_main, p_main, lb_logits, norm_w)


def _merge_kernel(x_ref, odn_ref, ohg_ref, gdn_ref, ghg_ref, wdn_ref, whg_ref, wo_ref, o_ref):
    a = _dot(odn_ref[...].astype(BF16), wdn_ref[...])
    b = _dot(ohg_ref[...].astype(BF16), whg_ref[...])
    merged = _sigmoid(gdn_ref[...].astype(F32)) * a + _sigmoid(ghg_ref[...].astype(F32)) * b
    o_ref[...] = x_ref[...] + _dot(merged.astype(BF16), wo_ref[...])


def _merge_out(x2d, o_dn, o_hg, p_main, w_dn, w_hg, w_out):
    t = x2d.shape[0]
    tm = min(1024, t)
    const = lambda shape: pl.BlockSpec(shape, lambda i: (0, 0))
    return pl.pallas_call(
        _merge_kernel,
        grid=(t // tm,),
        in_specs=[
            pl.BlockSpec((tm, D_MODEL), lambda i: (i, 0)),
            pl.BlockSpec((tm, MIX_WIDTH), lambda i: (i, 0)),
            pl.BlockSpec((tm, MIX_WIDTH), lambda i: (i, 0)),
            pl.BlockSpec((tm, D_MODEL), lambda i: (i, T_GATE_DN)),
            pl.BlockSpec((tm, D_MODEL), lambda i: (i, T_GATE_HG)),
            const((MIX_WIDTH, D_MODEL)), const((MIX_WIDTH, D_MODEL)), const((D_MODEL, D_MODEL)),
        ],
        out_specs=pl.BlockSpec((tm, D_MODEL), lambda i: (i, 0)),
        out_shape=jax.ShapeDtypeStruct((t, D_MODEL), F32),
        compiler_params=pltpu.CompilerParams(
            dimension_semantics=("arbitrary",), vmem_limit_bytes=VMEM_LIMIT),
        name="merge_out",
    )(x2d, o_dn, o_hg, p_main, p_main, w_dn, w_hg, w_out)


FF_COLS = 256


def _ffn_kernel(final, xm_ref, xp_ref, xn_ref, nw_ref, wg_ref, wu_ref, cw_ref, cb_ref, wd_ref, fn_ref,
                o_ref, g_scr):
    i = pl.program_id(1)
    last = pl.num_programs(1) - 1
    ts = xm_ref.shape[1]
    x = xm_ref[0]
    nw = nw_ref[...]
    h = _rms(x, nw).astype(BF16)
    hh = _rms(jnp.concatenate([xp_ref[0], xn_ref[0]], axis=0), nw).astype(BF16)
    acc = x
    for c in range(D_FF // FF_COLS):
        cols = slice(c * FF_COLS, (c + 1) * FF_COLS)
        wg = wg_ref[:, cols]
        g = _dot(h, wg)
        gh = _dot(hh, wg)
        g_scr[8:8 + ts, :] = g
        g_scr[7:8, :] = jnp.where(i > 0, gh[7:8, :], 0.0)
        g_scr[8 + ts:9 + ts, :] = jnp.where(i < last, gh[8:9, :], 0.0)
        gc = (cw_ref[0:1, cols] * g_scr[7:7 + ts, :] + cw_ref[1:2, cols] * g
              + cw_ref[2:3, cols] * g_scr[9:9 + ts, :] + cb_ref[:, cols])
        u = _dot(h, wu_ref[:, cols])
        act = (gc * _sigmoid(gc)) * u
        acc = acc + _dot(act.astype(BF16), wd_ref[cols, :])
    if final:
        acc = _rms(acc, fn_ref[...])
    o_ref[0] = acc


def _ffn(x3d, norm_w, w_gate, w_upp, conv_w, conv_b, w_down, final_w, final):
    batch, s_len, _ = x3d.shape
    ts = min(512, s_len)
    r8 = ts // 8
    n8 = s_len // 8
    const = lambda shape: pl.BlockSpec(shape, lambda b, i: (0, 0), pipeline_mode=pl.Buffered(1))
    return pl.pallas_call(
        functools.partial(_ffn_kernel, final),
        grid=(batch, s_len // ts),
        in_specs=[
            pl.BlockSpec((1, ts, D_MODEL), lambda b, i: (b, i, 0)),
            pl.BlockSpec((1, 8, D_MODEL), lambda b, i: (b, jnp.maximum(i * r8 - 1, 0), 0)),
            pl.BlockSpec((1, 8, D_MODEL), lambda b, i: (b, jnp.minimum((i + 1) * r8, n8 - 1), 0)),
            const((1, D_MODEL)),
            const((D_MODEL, D_FF)), const((D_MODEL, D_FF)),
            const((3, D_FF)), const((1, D_FF)),
            const((D_FF, D_MODEL)),
            const((1, D_MODEL)),
        ],
        out_specs=pl.BlockSpec((1, ts, D_MODEL), lambda b, i: (b, i, 0)),
        out_shape=jax.ShapeDtypeStruct(x3d.shape, F32),
        scratch_shapes=[pltpu.VMEM((ts + 16, FF_COLS), F32)],
        compiler_params=pltpu.CompilerParams(
            dimension_semantics=("arbitrary", "arbitrary"), vmem_limit_bytes=VMEM_LIMIT),
        name="ffn",
    )(x3d, x3d, x3d, norm_w, w_gate, w_upp, conv_w, conv_b, w_down, final_w)


def kernel(x, mix_norm, w_in, dn_conv, dn_a_log, dn_dt_bias, dn_norm, hg_lb_logits, hg_norm,
           w_branch_dn, w_branch_hg, w_out, ffn_norm, w_up, ffn_conv, ffn_conv_bias, w_down, final_norm):
    batch, s_len, _ = x.shape
    depth = mix_norm.shape[0]
    nc = s_len // CHUNK
    x2d = x.reshape(batch * s_len, D_MODEL)
    final_w = final_norm.reshape(1, D_MODEL)
    for l in range(depth):
        w = w_in[l]
        w_main = jnp.concatenate(
            [w[:, _C_GATE:], w[:, :_C_Z_END], w[:, _C_HQ:_C_GATE]], axis=1).astype(BF16)
        w_s = w[:, _C_Z_END:_C_HQ].astype(BF16)
        w_small = jnp.pad(w_s, ((0, 0), (0, LANES - N_SMALL)))
        p_main, small, small_t = _inproj(x2d, mix_norm[l].reshape(1, D_MODEL), w_main, w_small, w_s.T)
        small_t = small_t.reshape(N_SMALL, batch, s_len // DN_SUPER, DN_SUPER).transpose(1, 0, 2, 3)
        o_dn = _deltanet(p_main, small, small_t, dn_conv[l], dn_a_log[l], dn_dt_bias[l],
                         dn_norm[l].reshape(1, HEAD_DIM), batch, s_len)
        o_hg = _hgrn2(p_main, hg_lb_logits, hg_norm[l].reshape(1, HEAD_DIM), l, batch, s_len)
        x2d = _merge_out(x2d, o_dn, o_hg, p_main, w_branch_dn[l].astype(BF16),
                         w_branch_hg[l].astype(BF16), w_out[l].astype(BF16))
        wu = w_up[l].astype(BF16)
        x3d = _ffn(x2d.reshape(batch, s_len, D_MODEL), ffn_norm[l].reshape(1, D_MODEL),
                   wu[:, :D_FF], wu[:, D_FF:], ffn_conv[l], ffn_conv_bias[l].reshape(1, D_FF),
                   w_down[l].astype(BF16), final_w, l == depth - 1)
        x2d = x3d.reshape(batch * s_len, D_MODEL)
    return x2d.reshape(batch, s_len, D_MODEL)
```

```python
import functools

import jax
import jax.numpy as jnp
from jax import lax
from jax.experimental import pallas as pl
from jax.experimental.pallas import tpu as pltpu

F32 = jnp.float32
BF16 = jnp.bfloat16

LANES = 128
D_MODEL = 1024
N_HEADS = 4
HEAD_DIM = 128
MIX_WIDTH = N_HEADS * HEAD_DIM
DN_CONV = 5
D_FF = 2816
EPS = 1e-6
CHUNK = 64
SUB = 16
SUB_SHIFT = SUB.bit_length() - 1
HG_SUPER = 256
HG_SCAN_UNROLL = 4
HG_SAFE_EXP = 60.0
CHUNK_SHIFT = CHUNK.bit_length() - 1
DN_SUPER = 256
DN_UNROLL = 2
ACT_DTYPE = BF16
VMEM_LIMIT = 56 * 1024 * 1024

N_MAIN = 2 * D_MODEL + 4 * MIX_WIDTH + 5 * MIX_WIDTH
T_GATE_DN, T_GATE_HG = 0, 1
T_Q, T_K, T_V, T_Z = 16, 20, 24, 28
T_HQ, T_HFF, T_HFB, T_HI, T_HGATE = 32, 36, 40, 44, 48
N_SMALL = 16

_C_Z_END = 4 * MIX_WIDTH
_C_HQ = _C_Z_END + N_SMALL
_C_GATE = _C_HQ + 5 * MIX_WIDTH


def _sigmoid(x):
    return 0.5 * jnp.tanh(0.5 * x) + 0.5


def _log1p_nonneg(x):
    return jnp.log(1.0 + x)


def _softplus(x):
    return jnp.maximum(x, 0.0) + _log1p_nonneg(jnp.exp(-jnp.abs(x)))


def _rms(x, w):
    return x * lax.rsqrt(jnp.mean(x * x, axis=-1, keepdims=True) + EPS) * w


def _dot(a, b):
    return jnp.dot(a, b, preferred_element_type=F32)


def _dot_nt(a, b):
    return lax.dot_general(a, b, (((1,), (1,)), ((), ())), preferred_element_type=F32)


def _dot_tn(a, b):
    return lax.dot_general(a, b, (((0,), (0,)), ((), ())), preferred_element_type=F32)


def _split3(x):
    hi = x.astype(BF16)
    r = x - hi.astype(F32)
    mid = r.astype(BF16)
    lo = (r - mid.astype(F32)).astype(BF16)
    return hi, mid, lo


def _sel_left(m16, x):
    w = x.shape[1]
    y = _dot(m16, jnp.concatenate(_split3(x), axis=1))
    return y[:, :w] + y[:, w:2 * w] + y[:, 2 * w:]


def _sel_right(x, m16):
    r = x.shape[0]
    y = _dot(jnp.concatenate(_split3(x), axis=0), m16)
    return y[:r] + y[r:2 * r] + y[2 * r:]


def _tri(n, lower, strict=False):
    r = lax.broadcasted_iota(jnp.int32, (n, n), 0)
    c = lax.broadcasted_iota(jnp.int32, (n, n), 1)
    if lower:
        return (r > c) if strict else (r >= c)
    return (r < c) if strict else (r <= c)


def _inproj_kernel(x_ref, nw_ref, w_ref, ws_ref, wst_ref, p_ref, s_ref, st_ref, h_ref):
    @pl.when(pl.program_id(1) == 0)
    def _():
        hb = _rms(x_ref[...], nw_ref[...]).astype(BF16)
        h_ref[...] = hb
        s_ref[...] = _dot(hb, ws_ref[...])
        st_ref[...] = _dot_nt(wst_ref[...], hb)

    p_ref[...] = _dot(h_ref[...], w_ref[...]).astype(p_ref.dtype)


def _inproj(x2d, norm_w, w_main, w_small, w_small_t):
    t = x2d.shape[0]
    tm = min(1024, t)
    tn = N_MAIN // 4
    return pl.pallas_call(
        _inproj_kernel,
        grid=(t // tm, N_MAIN // tn),
        in_specs=[
            pl.BlockSpec((tm, D_MODEL), lambda i, j: (i, 0)),
            pl.BlockSpec((1, D_MODEL), lambda i, j: (0, 0)),
            pl.BlockSpec((D_MODEL, tn), lambda i, j: (0, j)),
            pl.BlockSpec((D_MODEL, LANES), lambda i, j: (0, 0)),
            pl.BlockSpec((N_SMALL, D_MODEL), lambda i, j: (0, 0)),
        ],
        out_specs=[
            pl.BlockSpec((tm, tn), lambda i, j: (i, j)),
            pl.BlockSpec((tm, LANES), lambda i, j: (i, 0)),
            pl.BlockSpec((N_SMALL, tm), lambda i, j: (0, i)),
        ],
        out_shape=[
            jax.ShapeDtypeStruct((t, N_MAIN), ACT_DTYPE),
            jax.ShapeDtypeStruct((t, LANES), F32),
            jax.ShapeDtypeStruct((N_SMALL, t), F32),
        ],
        scratch_shapes=[pltpu.VMEM((tm, D_MODEL), BF16)],
        compiler_params=pltpu.CompilerParams(
            dimension_semantics=("arbitrary", "arbitrary"), vmem_limit_bytes=VMEM_LIMIT),
        name="inproj",
    )(x2d, norm_w, w_main, w_small, w_small_t)


def _neumann_inverse(a, block):
    n = a.shape[0]
    eye = (lax.broadcasted_iota(jnp.int32, (n, n), 0) == lax.broadcasted_iota(jnp.int32, (n, n), 1)).astype(F32)
    steps = block.bit_length() - 1
    a16 = a.astype(BF16)
    s = eye - a
    x = _dot(a16, a16)
    for k in range(1, steps):
        x16 = x.astype(BF16)
        s = s + _dot(s.astype(BF16), x16)
        if k < steps - 1:
            x = _dot(x16, x16)
    return s


def _dn_kernel(alog_ref, dtb_ref, q_ref, k_ref, v_ref, z_ref, small_ref, smallt_ref,
               cq_ref, ck_ref, cv_ref, nw_ref, o_ref,
               pad_scr, qs, ks, vs, ball, gcall, bcol, gcol, grow, st_scr, of_scr, mt_scr, nt_scr, qh_scr,
               gl_scr):
    h = pl.program_id(1)
    s_len = q_ref.shape[0]
    nc = s_len // CHUNK
    lane = lax.broadcasted_iota(jnp.int32, (1, LANES), 1)

    pad_scr[0:8, :] = jnp.zeros((8, LANES), F32)
    pad_scr[8 + s_len:16 + s_len, :] = jnp.zeros((8, LANES), F32)

    def conv_silu(x_ref, cw_ref):
        pad_scr[8:8 + s_len, :] = x_ref[...].astype(F32)
        y = cw_ref[0:1, :] * pad_scr[6:6 + s_len, :]
        for j in range(1, DN_CONV):
            y = y + cw_ref[j:j + 1, :] * pad_scr[6 + j:6 + j + s_len, :]
        return y * _sigmoid(y)

    def l2n(y):
        return y * lax.rsqrt(jnp.sum(y * y, axis=-1, keepdims=True) + EPS)

    qs[...] = l2n(conv_silu(q_ref, cq_ref)) * (HEAD_DIM ** -0.5)
    ks[...] = l2n(conv_silu(k_ref, ck_ref))
    vs[...] = conv_silu(v_ref, cv_ref)

    n_sup = s_len // DN_SUPER
    per_sup = DN_SUPER // CHUNK
    r_i = lax.broadcasted_iota(jnp.int32, (DN_SUPER, DN_SUPER), 0)
    c_i = lax.broadcasted_iota(jnp.int32, (DN_SUPER, DN_SUPER), 1)
    same = lax.shift_right_logical(r_i, CHUNK_SHIFT) == lax.shift_right_logical(c_i, CHUNK_SHIFT)
    incl = (same & (r_i >= c_i), same & (r_i <= c_i))
    strict = (same & (r_i > c_i), same & (r_i < c_i))

    @pl.when(h == 0)
    def _():
        alv = jnp.zeros((1, LANES), F32)
        dtv = jnp.zeros((1, LANES), F32)
        for d in range(2):
            for hh in range(N_HEADS):
                alv = jnp.where(lane == 8 + d * N_HEADS + hh, alog_ref[d, hh], alv)
                dtv = jnp.where(lane == 8 + d * N_HEADS + hh, dtb_ref[d, hh], dtv)
        sm = small_ref[...]
        ball[...] = _sigmoid(sm)
        gcall[...] = -jnp.exp(alv) * _softplus(sm + dtv)
        tri_both = jnp.concatenate([incl[0].astype(BF16), incl[1].astype(BF16)], axis=0)
        fwd_lane = lane < 8 + N_HEADS

        def cumulate(g, carry):
            rows = pl.ds(pl.multiple_of(g * DN_SUPER, DN_SUPER), DN_SUPER)
            both = _sel_left(tri_both, gcall[rows, :])
            gcall[rows, :] = jnp.where(fwd_lane, both[:DN_SUPER], both[DN_SUPER:])
            return carry

        lax.fori_loop(0, n_sup, cumulate, 0)

    def pick(ref, idx):
        col = jnp.sum(jnp.where(lane == idx, ref[...], 0.0), axis=-1, keepdims=True)
        return jnp.broadcast_to(col, (s_len, LANES))

    for d in range(2):
        bcol[d] = pick(ball, d * N_HEADS + h)
        gcol[d] = pick(gcall, 8 + d * N_HEADS + h)

    for d in range(2):
        al = jnp.full((1, DN_SUPER), alog_ref[d, h], F32)
        dt = jnp.full((1, DN_SUPER), dtb_ref[d, h], F32)
        g_r = -jnp.exp(al) * _softplus(smallt_ref[0, 8 + d * N_HEADS + h] + dt)
        grow[d] = _sel_right(g_r, incl[1 - d].astype(BF16))

    row_blk = lax.shift_right_logical(lax.broadcasted_iota(jnp.int32, (DN_SUPER, 1), 0), CHUNK_SHIFT)

    def prepare(g, carry):
        rows = pl.ds(pl.multiple_of(g * DN_SUPER, DN_SUPER), DN_SUPER)
        q = qs[rows, :]
        k = ks[rows, :]
        v = vs[rows, :]
        k16 = k.astype(BF16)
        kk = _dot_nt(k16, k16)
        qk_raw = _dot_nt(q.astype(BF16), k16)
        o_local = None
        for d in range(2):
            gc_c = gcol[d, rows, :]
            be_c = bcol[d, rows, :]
            gc_r = grow[d, pl.ds(g, 1), :]
            diff = jnp.concatenate([gc_c, gc_c], axis=1) - gc_r
            decay = jnp.where(incl[d], jnp.exp(jnp.where(incl[d], diff, 0.0)), 0.0)
            a = jnp.where(strict[d], kk * jnp.concatenate([be_c, be_c], axis=1) * decay, 0.0)
            t_inv = _neumann_inverse(a, CHUNK)
            e_gc = jnp.exp(gc_c)
            rhs = jnp.concatenate([v * be_c, k * (be_c * e_gc)], axis=1).astype(BF16)
            uw16 = _dot(t_inv.astype(BF16), rhs).astype(BF16)
            qk16 = (qk_raw * decay).astype(BF16)
            qu_qw = _dot(qk16, uw16)
            last = CHUNK - 1 if d == 0 else 0
            gc_last = [gc_c[j * CHUNK + last:j * CHUNK + last + 1, :] for j in range(per_sup)]
            gcl = jnp.concatenate([jnp.broadcast_to(x, (CHUNK, LANES)) for x in gc_last], axis=0)
            kd = k * jnp.exp(gcl - gc_c)
            kd_bd = jnp.concatenate(
                [jnp.where(row_blk == j, kd, 0.0) for j in range(per_sup)], axis=1).astype(BF16)
            mn = _dot_tn(uw16, kd_bd)
            for j in range(per_sup):
                c = g * per_sup + j
                cols = slice(j * HEAD_DIM, (j + 1) * HEAD_DIM)
                nt_scr[d, c] = mn[:HEAD_DIM, cols]
                mt_scr[d, c] = (-mn[HEAD_DIM:, cols]).astype(BF16)
                gl_scr[d, c] = jnp.exp(gc_last[j])
            qh_scr[d, rows, :] = (q * e_gc - qu_qw[:, HEAD_DIM:]).astype(BF16)
            o_local = qu_qw[:, :HEAD_DIM] if d == 0 else o_local + qu_qw[:, :HEAD_DIM]
        of_scr[rows, :] = o_local
        return carry

    def prepare_pair(gg, carry):
        for j in range(DN_UNROLL):
            prepare(gg * DN_UNROLL + j, carry)
        return carry

    lax.fori_loop(0, n_sup // DN_UNROLL, prepare_pair, 0)

    st_scr[...] = jnp.zeros(st_scr.shape, F32)

    def scan_step(i, carry):
        for d in range(2):
            c = i if d == 0 else nc - 1 - i
            rows = pl.ds(pl.multiple_of(c * CHUNK, CHUNK), CHUNK)
            st = st_scr[d]
            st16 = st.astype(BF16)
            of_scr[rows, :] += _dot_nt(qh_scr[d, rows, :], st16)
            st_scr[d] = st * gl_scr[d, c] + _dot(st16, mt_scr[d, c]) + nt_scr[d, c]
        return carry

    lax.fori_loop(0, nc, scan_step, 0)

    o = of_scr[...]
    y = o * lax.rsqrt(jnp.mean(o * o, axis=-1, keepdims=True) + EPS) * nw_ref[...]
    zz = z_ref[...].astype(F32)
    o_ref[...] = (y * (zz * _sigmoid(zz))).astype(o_ref.dtype)


def _deltanet(p_main, small, small_t, conv_w, a_log, dt_bias, norm_w, batch, s_len):
    nc = s_len // CHUNK
    n_sup = s_len // DN_SUPER
    assert s_len % DN_SUPER == 0
    tok = lambda tile: pl.BlockSpec((s_len, LANES), lambda b, h, tile=tile: (b, tile + h))
    cw = lambda tile: pl.BlockSpec((DN_CONV, LANES), lambda b, h, tile=tile: (0, tile + h))
    smem = pl.BlockSpec(memory_space=pltpu.SMEM)
    return pl.pallas_call(
        _dn_kernel,
        grid=(batch, N_HEADS),
        in_specs=[
            smem, smem,
            tok(T_Q), tok(T_K), tok(T_V), tok(T_Z),
            pl.BlockSpec((s_len, LANES), lambda b, h: (b, 0)),
            pl.BlockSpec((1, N_SMALL, n_sup, DN_SUPER), lambda b, h: (b, 0, 0, 0)),
            cw(0), cw(N_HEADS), cw(2 * N_HEADS),
            pl.BlockSpec((1, LANES), lambda b, h: (0, 0)),
        ],
        out_specs=pl.BlockSpec((s_len, LANES), lambda b, h: (b, h)),
        out_shape=jax.ShapeDtypeStruct((batch * s_len, MIX_WIDTH), ACT_DTYPE),
        scratch_shapes=[
            pltpu.VMEM((s_len + 16, LANES), F32),
            pltpu.VMEM((s_len, LANES), F32),
            pltpu.VMEM((s_len, LANES), F32),
            pltpu.VMEM((s_len, LANES), F32),
            pltpu.VMEM((s_len, LANES), F32),
            pltpu.VMEM((s_len, LANES), F32),
            pltpu.VMEM((2, s_len, LANES), F32),
            pltpu.VMEM((2, s_len, LANES), F32),
            pltpu.VMEM((2, n_sup, DN_SUPER), F32),
            pltpu.VMEM((2, HEAD_DIM, HEAD_DIM), F32),
            pltpu.VMEM((s_len, LANES), F32),
            pltpu.VMEM((2, nc, HEAD_DIM, HEAD_DIM), BF16),
            pltpu.VMEM((2, nc, HEAD_DIM, HEAD_DIM), F32),
            pltpu.VMEM((2, s_len, LANES), BF16),
            pltpu.VMEM((2, nc, 1, LANES), F32),
        ],
        compiler_params=pltpu.CompilerParams(
            dimension_semantics=("arbitrary", "arbitrary"), vmem_limit_bytes=VMEM_LIMIT),
        name="deltanet",
    )(a_log, dt_bias, p_main, p_main, p_main, p_main, small, small_t,
      conv_w, conv_w, conv_w, norm_w)


def _hg_kernel(layer, hq_ref, hff_ref, hfb_ref, hi_ref, hg_ref, lbl_ref, nw_ref, o_ref,
               qs, iv, lf, kk, st_scr, oi_scr, nt_scr, fl_scr, qt_scr):
    s_len = hq_ref.shape[0]
    nc = s_len // CHUNK

    lg = lbl_ref[...].astype(F32)
    e = jnp.exp(lg - jnp.max(lg, axis=0, keepdims=True))
    sm = e / jnp.sum(e, axis=0, keepdims=True)
    lb = jnp.clip(jnp.sum(sm[:layer + 1], axis=0) - sm[0], 0.0, 1.0)

    x = hq_ref[...].astype(F32)
    qs[...] = (x * _sigmoid(x)) * (HEAD_DIM ** -0.5)
    iv[...] = hi_ref[...].astype(BF16)
    for d, ref in enumerate((hff_ref, hfb_ref)):
        fr = ref[...].astype(F32)
        lbd = lb[d:d + 1, :]
        log_sig = jnp.minimum(fr, 0.0) - _log1p_nonneg(jnp.exp(-jnp.abs(fr)))
        lf[d] = log_sig + _log1p_nonneg(lbd * jnp.exp(-fr))
        kk[d] = (1.0 - lbd) * _sigmoid(-fr)

    n_sup = s_len // HG_SUPER
    per_sup = HG_SUPER // CHUNK
    subs = CHUNK // SUB
    r_i = lax.broadcasted_iota(jnp.int32, (HG_SUPER, HG_SUPER), 0)
    c_i = lax.broadcasted_iota(jnp.int32, (HG_SUPER, HG_SUPER), 1)
    same_chunk = lax.shift_right_logical(r_i, CHUNK_SHIFT) == lax.shift_right_logical(c_i, CHUNK_SHIFT)
    r_sub = lax.shift_right_logical(r_i, SUB_SHIFT)
    c_sub = lax.shift_right_logical(c_i, SUB_SHIFT)
    causal = (r_i >= c_i, r_i <= c_i)
    row_i = lax.broadcasted_iota(jnp.int32, (HG_SUPER, 1), 0)
    row_sub = lax.shift_right_logical(row_i, SUB_SHIFT) & (subs - 1)
    row_blk = lax.shift_right_logical(row_i, CHUNK_SHIFT)
    pos16 = lax.broadcasted_iota(jnp.int32, (1, SUB, 1), 1)

    def group_row(x, group, idx):
        x3 = x.reshape(x.shape[0] // group, group, LANES)
        return jnp.broadcast_to(x3[:, idx:idx + 1, :], x3.shape).reshape(x.shape)

    def prepare(g, carry):
        rows = pl.ds(pl.multiple_of(g * HG_SUPER, HG_SUPER), HG_SUPER)
        q = qs[rows, :]
        i16 = iv[rows, :]
        for d in range(2):
            first = 0 if d == 0 else SUB - 1
            last = CHUNK - 1 if d == 0 else 0
            k = kk[d, rows, :]
            tri = (same_chunk & causal[d]).astype(BF16)
            b = _sel_left(tri, lf[d, rows, :])
            ref16 = group_row(b, SUB, first)
            expo = ref16 - b
            qh = (q * jnp.exp(b - ref16)).astype(BF16)
            kh = (k * jnp.exp(jnp.minimum(expo, HG_SAFE_EXP))).astype(BF16)
            keys = [kh]
            masks = []
            for t_sub in (range(1, subs) if d == 0 else range(subs - 1)):
                ref_t = group_row(b, CHUNK, t_sub * SUB + first)
                src = (row_sub < t_sub) if d == 0 else (row_sub > t_sub)
                keys.append(jnp.where(src, k * jnp.exp(jnp.where(src, ref_t - b, 0.0)), 0.0).astype(BF16))
                masks.append(same_chunk & ((r_sub & (subs - 1)) == t_sub) & (
                    ((c_sub & (subs - 1)) < t_sub) if d == 0 else ((c_sub & (subs - 1)) > t_sub)))
            att_all = _dot_nt(qh, jnp.concatenate(keys, axis=0))
            att_diag = jnp.where((r_sub == c_sub) & causal[d], att_all[:, :HG_SUPER], 0.0)
            att_off = jnp.zeros((HG_SUPER, HG_SUPER), F32)
            for j, m in enumerate(masks):
                att_off = att_off + jnp.where(m, att_all[:, (j + 1) * HG_SUPER:(j + 2) * HG_SUPER], 0.0)
            oi_scr[d, rows, :] = _dot((att_off + att_diag).astype(BF16), i16)

            @pl.when(jnp.max(expo) > HG_SAFE_EXP)
            def _(b=b, k=k, att_off=att_off, d=d):
                b3 = b.reshape(HG_SUPER // SUB, SUB, LANES)
                q3 = q.reshape(b3.shape)
                k3 = k.reshape(b3.shape)
                i3 = i16.astype(F32).reshape(b3.shape)
                acc = jnp.zeros(b3.shape, F32)
                for s in range(SUB):
                    ok = (pos16 >= s) if d == 0 else (pos16 <= s)
                    ex = jnp.exp(jnp.where(ok, b3 - b3[:, s:s + 1, :], 0.0))
                    col = jnp.sum(q3 * k3[:, s:s + 1, :] * ex, axis=-1, keepdims=True)
                    acc = acc + jnp.where(ok, col, 0.0) * i3[:, s:s + 1, :]
                oi_scr[d, rows, :] = _dot(att_off.astype(BF16), i16) + acc.reshape(HG_SUPER, LANES)

            b_last = group_row(b, CHUNK, last)
            qt_scr[d, rows, :] = (q * jnp.exp(b)).astype(BF16)
            kt = k * jnp.exp(b_last - b)
            kt_bd = jnp.concatenate(
                [jnp.where(row_blk == j, kt, 0.0) for j in range(per_sup)], axis=1).astype(BF16)
            nts = _dot_tn(i16, kt_bd)
            for j in range(per_sup):
                c = g * per_sup + j
                nt_scr[d, c] = nts[:, j * HEAD_DIM:(j + 1) * HEAD_DIM]
                fl_scr[d, c] = jnp.exp(b_last[j * CHUNK:j * CHUNK + 1, :])
        return carry

    lax.fori_loop(0, n_sup, prepare, 0)

    st_scr[...] = jnp.zeros(st_scr.shape, F32)

    def scan_step(ii, carry):
        for u in range(HG_SCAN_UNROLL):
            i = ii * HG_SCAN_UNROLL + u
            for d in range(2):
                c = i if d == 0 else nc - 1 - i
                rows = pl.ds(pl.multiple_of(c * CHUNK, CHUNK), CHUNK)
                st = st_scr[d]
                oi_scr[d, rows, :] += _dot_nt(qt_scr[d, rows, :], st.astype(BF16))
                st_scr[d] = st * fl_scr[d, c] + nt_scr[d, c]
        return carry

    lax.fori_loop(0, nc // HG_SCAN_UNROLL, scan_step, 0)

    o = oi_scr[0] + oi_scr[1]
    y = o * lax.rsqrt(jnp.mean(o * o, axis=-1, keepdims=True) + EPS) * nw_ref[...]
    zz = hg_ref[...].astype(F32)
    o_ref[...] = (y * (zz * _sigmoid(zz))).astype(o_ref.dtype)


def _hgrn2(p_main, lb_logits, norm_w, layer, batch, s_len):
    nc = s_len // CHUNK
    assert s_len % HG_SUPER == 0
    depth = lb_logits.shape[0]
    tok = lambda tile: pl.BlockSpec((s_len, LANES), lambda b, h, tile=tile: (b, tile + h))
    return pl.pallas_call(
        functools.partial(_hg_kernel, layer),
        grid=(batch, N_HEADS),
        in_specs=[
            tok(T_HQ), tok(T_HFF), tok(T_HFB), tok(T_HI), tok(T_HGATE),
            pl.BlockSpec((depth, 2, LANES), lambda b, h: (0, 0, h)),
            pl.BlockSpec((1, LANES), lambda b, h: (0, 0)),
        ],
        out_specs=pl.BlockSpec((s_len, LANES), lambda b, h: (b, h)),
        out_shape=jax.ShapeDtypeStruct((batch * s_len, MIX_WIDTH), ACT_DTYPE),
        scratch_shapes=[
            pltpu.VMEM((s_len, LANES), F32),
            pltpu.VMEM((s_len, LANES), BF16),
            pltpu.VMEM((2, s_len, LANES), F32),
            pltpu.VMEM((2, s_len, LANES), F32),
            pltpu.VMEM((2, HEAD_DIM, HEAD_DIM), F32),
            pltpu.VMEM((2, s_len, LANES), F32),
            pltpu.VMEM((2, nc, HEAD_DIM, HEAD_DIM), F32),
            pltpu.VMEM((2, nc, 1, LANES), F32),
            pltpu.VMEM((2, s_len, LANES), BF16),
        ],
        compiler_params=pltpu.CompilerParams(
            dimension_semantics=("arbitrary", "arbitrary"), vmem_limit_bytes=VMEM_LIMIT),
        name="hgrn2",
    )(p_main, p_main, p_main, p_main, p_main, lb_logits, norm_w)


def _merge_kernel(x_ref, odn_ref, ohg_ref, gdn_ref, ghg_ref, wdn_ref, whg_ref, wo_ref, o_ref):
    a = _dot(odn_ref[...].astype(BF16), wdn_ref[...])
    b = _dot(ohg_ref[...].astype(BF16), whg_ref[...])
    merged = _sigmoid(gdn_ref[...].astype(F32)) * a + _sigmoid(ghg_ref[...].astype(F32)) * b
    o_ref[...] = x_ref[...] + _dot(merged.astype(BF16), wo_ref[...])


def _merge_out(x2d, o_dn, o_hg, p_main, w_dn, w_hg, w_out):
    t = x2d.shape[0]
    tm = min(1024, t)
    const = lambda shape: pl.BlockSpec(shape, lambda i: (0, 0))
    return pl.pallas_call(
        _merge_kernel,
        grid=(t // tm,),
        in_specs=[
            pl.BlockSpec((tm, D_MODEL), lambda i: (i, 0)),
            pl.BlockSpec((tm, MIX_WIDTH), lambda i: (i, 0)),
            pl.BlockSpec((tm, MIX_WIDTH), lambda i: (i, 0)),
            pl.BlockSpec((tm, D_MODEL), lambda i: (i, T_GATE_DN)),
            pl.BlockSpec((tm, D_MODEL), lambda i: (i, T_GATE_HG)),
            const((MIX_WIDTH, D_MODEL)), const((MIX_WIDTH, D_MODEL)), const((D_MODEL, D_MODEL)),
        ],
        out_specs=pl.BlockSpec((tm, D_MODEL), lambda i: (i, 0)),
        out_shape=jax.ShapeDtypeStruct((t, D_MODEL), F32),
        compiler_params=pltpu.CompilerParams(
            dimension_semantics=("arbitrary",), vmem_limit_bytes=VMEM_LIMIT),
        name="merge_out",
    )(x2d, o_dn, o_hg, p_main, p_main, w_dn, w_hg, w_out)


FF_COLS = D_FF // 2


def _ffn_kernel(final, xm_ref, xp_ref, xn_ref, nw_ref, wg_ref, wu_ref, cw_ref, cb_ref, wd_ref, fn_ref,
                o_ref, g_scr):
    i = pl.program_id(1)
    last = pl.num_programs(1) - 1
    ts = xm_ref.shape[1]
    x = xm_ref[0]
    h_ext = _rms(jnp.concatenate([xp_ref[0], x, xn_ref[0]], axis=0), nw_ref[...])
    h = h_ext[8:8 + ts].astype(BF16)
    h_ext = h_ext.astype(BF16)
    acc = x
    for c in range(D_FF // FF_COLS):
        cols = slice(c * FF_COLS, (c + 1) * FF_COLS)
        g_scr[...] = _dot(h_ext, wg_ref[:, cols])
        g_scr[7:8, :] = jnp.where(i > 0, g_scr[7:8, :], 0.0)
        g_scr[8 + ts:9 + ts, :] = jnp.where(i < last, g_scr[8 + ts:9 + ts, :], 0.0)
        gc = (cw_ref[0:1, cols] * g_scr[7:7 + ts, :] + cw_ref[1:2, cols] * g_scr[8:8 + ts, :]
              + cw_ref[2:3, cols] * g_scr[9:9 + ts, :] + cb_ref[:, cols])
        u = _dot(h, wu_ref[:, cols])
        act = (gc * _sigmoid(gc)) * u
        acc = acc + _dot(act.astype(BF16), wd_ref[cols, :])
    if final:
        acc = _rms(acc, fn_ref[...])
    o_ref[0] = acc


def _ffn(x3d, norm_w, w_gate, w_upp, conv_w, conv_b, w_down, final_w, final):
    batch, s_len, _ = x3d.shape
    ts = min(512, s_len)
    r8 = ts // 8
    n8 = s_len // 8
    const = lambda shape: pl.BlockSpec(shape, lambda b, i: (0, 0), pipeline_mode=pl.Buffered(1))
    return pl.pallas_call(
        functools.partial(_ffn_kernel, final),
        grid=(batch, s_len // ts),
        in_specs=[
            pl.BlockSpec((1, ts, D_MODEL), lambda b, i: (b, i, 0)),
            pl.BlockSpec((1, 8, D_MODEL), lambda b, i: (b, jnp.maximum(i * r8 - 1, 0), 0)),
            pl.BlockSpec((1, 8, D_MODEL), lambda b, i: (b, jnp.minimum((i + 1) * r8, n8 - 1), 0)),
            const((1, D_MODEL)),
            const((D_MODEL, D_FF)), const((D_MODEL, D_FF)),
            const((3, D_FF)), const((1, D_FF)),
            const((D_FF, D_MODEL)),
            const((1, D_MODEL)),
        ],
        out_specs=pl.BlockSpec((1, ts, D_MODEL), lambda b, i: (b, i, 0)),
        out_shape=jax.ShapeDtypeStruct(x3d.shape, F32),
        scratch_shapes=[pltpu.VMEM((ts + 16, FF_COLS), F32)],
        compiler_params=pltpu.CompilerParams(
            dimension_semantics=("arbitrary", "arbitrary"), vmem_limit_bytes=VMEM_LIMIT),
        name="ffn",
    )(x3d, x3d, x3d, norm_w, w_gate, w_upp, conv_w, conv_b, w_down, final_w)


def kernel(x, mix_norm, w_in, dn_conv, dn_a_log, dn_dt_bias, dn_norm, hg_lb_logits, hg_norm,
           w_branch_dn, w_branch_hg, w_out, ffn_norm, w_up, ffn_conv, ffn_conv_bias, w_down, final_norm):
    batch, s_len, _ = x.shape
    depth = mix_norm.shape[0]
    nc = s_len // CHUNK
    x2d = x.reshape(batch * s_len, D_MODEL)
    final_w = final_norm.reshape(1, D_MODEL)
    for l in range(depth):
        w = w_in[l]
        w_main = jnp.concatenate(
            [w[:, _C_GATE:], w[:, :_C_Z_END], w[:, _C_HQ:_C_GATE]], axis=1).astype(BF16)
        w_s = w[:, _C_Z_END:_C_HQ].astype(BF16)
        w_small = jnp.pad(w_s, ((0, 0), (0, LANES - N_SMALL)))
        p_main, small, small_t = _inproj(x2d, mix_norm[l].reshape(1, D_MODEL), w_main, w_small, w_s.T)
        small_t = small_t.reshape(N_SMALL, batch, s_len // DN_SUPER, DN_SUPER).transpose(1, 0, 2, 3)
        o_dn = _deltanet(p_main, small, small_t, dn_conv[l], dn_a_log[l], dn_dt_bias[l],
                         dn_norm[l].reshape(1, HEAD_DIM), batch, s_len)
        o_hg = _hgrn2(p_main, hg_lb_logits, hg_norm[l].reshape(1, HEAD_DIM), l, batch, s_len)
        x2d = _merge_out(x2d, o_dn, o_hg, p_main, w_branch_dn[l].astype(BF16),
                         w_branch_hg[l].astype(BF16), w_out[l].astype(BF16))
        wu = w_up[l].astype(BF16)
        x3d = _ffn(x2d.reshape(batch, s_len, D_MODEL), ffn_norm[l].reshape(1, D_MODEL),
                   wu[:, :D_FF], wu[:, D_FF:], ffn_conv[l], ffn_conv_bias[l].reshape(1, D_FF),
                   w_down[l].astype(BF16), final_w, l == depth - 1)
        x2d = x3d.reshape(batch * s_len, D_MODEL)
    return x2d.reshape(batch, s_len, D_MODEL)
```

```python
import functools

import jax
import jax.numpy as jnp
from jax import lax
from jax.experimental import pallas as pl
from jax.experimental.pallas import tpu as pltpu

F32 = jnp.float32
BF16 = jnp.bfloat16

LANES = 128
D_MODEL = 1024
N_HEADS = 4
HEAD_DIM = 128
MIX_WIDTH = N_HEADS * HEAD_DIM
DN_CONV = 5
D_FF = 2816
EPS = 1e-6
CHUNK = 64
SUB = 16
SUB_SHIFT = SUB.bit_length() - 1
HG_SUPER = 256
HG_PAIR = 2
HG_SCAN_UNROLL = 4
HG_SAFE_EXP = 60.0
CHUNK_SHIFT = CHUNK.bit_length() - 1
DN_SUPER = 256
DN_PAIR = 2
ACT_DTYPE = BF16
VMEM_LIMIT = 56 * 1024 * 1024

N_MAIN = 2 * D_MODEL + 4 * MIX_WIDTH + 5 * MIX_WIDTH
T_GATE_DN, T_GATE_HG = 0, 1
T_Q, T_K, T_V, T_Z = 16, 20, 24, 28
T_HQ, T_HFF, T_HFB, T_HI, T_HGATE = 32, 36, 40, 44, 48
N_SMALL = 16

_C_Z_END = 4 * MIX_WIDTH
_C_HQ = _C_Z_END + N_SMALL
_C_GATE = _C_HQ + 5 * MIX_WIDTH


def _sigmoid(x):
    return 0.5 * jnp.tanh(0.5 * x) + 0.5


def _log1p_nonneg(x):
    return jnp.log(1.0 + x)


def _softplus(x):
    return jnp.maximum(x, 0.0) + _log1p_nonneg(jnp.exp(-jnp.abs(x)))


def _rms(x, w):
    return x * lax.rsqrt(jnp.mean(x * x, axis=-1, keepdims=True) + EPS) * w


def _dot(a, b):
    return jnp.dot(a, b, preferred_element_type=F32)


def _dot_nt(a, b):
    return lax.dot_general(a, b, (((1,), (1,)), ((), ())), preferred_element_type=F32)


def _dot_tn(a, b):
    return lax.dot_general(a, b, (((0,), (0,)), ((), ())), preferred_element_type=F32)


def _split3(x):
    hi = x.astype(BF16)
    r = x - hi.astype(F32)
    mid = r.astype(BF16)
    lo = (r - mid.astype(F32)).astype(BF16)
    return hi, mid, lo


def _sel_left(m16, x):
    w = x.shape[1]
    y = _dot(m16, jnp.concatenate(_split3(x), axis=1))
    return y[:, :w] + y[:, w:2 * w] + y[:, 2 * w:]


def _sel_right(x, m16):
    r = x.shape[0]
    y = _dot(jnp.concatenate(_split3(x), axis=0), m16)
    return y[:r] + y[r:2 * r] + y[2 * r:]


def _tri(n, lower, strict=False):
    r = lax.broadcasted_iota(jnp.int32, (n, n), 0)
    c = lax.broadcasted_iota(jnp.int32, (n, n), 1)
    if lower:
        return (r > c) if strict else (r >= c)
    return (r < c) if strict else (r <= c)


def _inproj_kernel(x_ref, nw_ref, w_ref, ws_ref, wst_ref, p_ref, s_ref, st_ref, h_ref):
    @pl.when(pl.program_id(1) == 0)
    def _():
        hb = _rms(x_ref[...], nw_ref[...]).astype(BF16)
        h_ref[...] = hb
        s_ref[...] = _dot(hb, ws_ref[...])
        st_ref[...] = _dot_nt(wst_ref[...], hb)

    p_ref[...] = _dot(h_ref[...], w_ref[...]).astype(p_ref.dtype)


def _inproj(x2d, norm_w, w_main, w_small, w_small_t):
    t = x2d.shape[0]
    tm = min(1024, t)
    tn = N_MAIN // 4
    return pl.pallas_call(
        _inproj_kernel,
        grid=(t // tm, N_MAIN // tn),
        in_specs=[
            pl.BlockSpec((tm, D_MODEL), lambda i, j: (i, 0)),
            pl.BlockSpec((1, D_MODEL), lambda i, j: (0, 0)),
            pl.BlockSpec((D_MODEL, tn), lambda i, j: (0, j)),
            pl.BlockSpec((D_MODEL, LANES), lambda i, j: (0, 0)),
            pl.BlockSpec((N_SMALL, D_MODEL), lambda i, j: (0, 0)),
        ],
        out_specs=[
            pl.BlockSpec((tm, tn), lambda i, j: (i, j)),
            pl.BlockSpec((tm, LANES), lambda i, j: (i, 0)),
            pl.BlockSpec((N_SMALL, tm), lambda i, j: (0, i)),
        ],
        out_shape=[
            jax.ShapeDtypeStruct((t, N_MAIN), ACT_DTYPE),
            jax.ShapeDtypeStruct((t, LANES), F32),
            jax.ShapeDtypeStruct((N_SMALL, t), F32),
        ],
        scratch_shapes=[pltpu.VMEM((tm, D_MODEL), BF16)],
        compiler_params=pltpu.CompilerParams(
            dimension_semantics=("arbitrary", "arbitrary"), vmem_limit_bytes=VMEM_LIMIT),
        name="inproj",
    )(x2d, norm_w, w_main, w_small, w_small_t)


def _neumann_inverses(mats, block):
    n = mats[0].shape[0]
    nb = n // block
    shift = block.bit_length() - 1
    steps = shift
    r_p = lax.broadcasted_iota(jnp.int32, (block, n), 0)
    c_p = lax.broadcasted_iota(jnp.int32, (block, n), 1)
    c_blk = lax.shift_right_logical(c_p, shift)
    r_n = lax.broadcasted_iota(jnp.int32, (n, n), 0)
    c_n = lax.broadcasted_iota(jnp.int32, (n, n), 1)
    same = lax.shift_right_logical(r_n, shift) == lax.shift_right_logical(c_n, shift)

    def pack(m):
        out = m[0:block]
        for j in range(1, nb):
            out = jnp.where(c_blk == j, m[j * block:(j + 1) * block], out)
        return out

    def unpack(p):
        return jnp.where(same, jnp.concatenate([p] * nb, axis=0), 0.0)

    eye_p = (r_p == (c_p & (block - 1))).astype(F32)
    xs = [pack(a) for a in mats]
    ss = [eye_p - x for x in xs]
    xs = [_dot(x.astype(BF16), a.astype(BF16)) for x, a in zip(xs, mats)]
    for k in range(1, steps):
        x_bd = [unpack(x).astype(BF16) for x in xs]
        if k < steps - 1:
            lhs = [jnp.concatenate([s, x], axis=0).astype(BF16) for s, x in zip(ss, xs)]
            both = [_dot(l, xb) for l, xb in zip(lhs, x_bd)]
            ss = [s + b[:block] for s, b in zip(ss, both)]
            xs = [b[block:] for b in both]
        else:
            s16 = [s.astype(BF16) for s in ss]
            ss = [s + _dot(l, xb) for s, l, xb in zip(ss, s16, x_bd)]
    return [unpack(s) for s in ss]


def _dn_kernel(alog_ref, dtb_ref, q_ref, k_ref, v_ref, z_ref, small_ref, smallt_ref,
               cq_ref, ck_ref, cv_ref, nw_ref, o_ref,
               qpad, kpad, vpad, stg, ball, gcall, grow, st_scr, of_scr, mt_scr, nt_scr, qh_scr,
               gl_scr):
    h = pl.program_id(1)
    s_len = q_ref.shape[0]
    nc = s_len // CHUNK
    lane = lax.broadcasted_iota(jnp.int32, (1, LANES), 1)

    for pad, x_ref in ((qpad, q_ref), (kpad, k_ref), (vpad, v_ref)):
        pad[0:8, :] = jnp.zeros((8, LANES), F32)
        pad[8 + s_len:16 + s_len, :] = jnp.zeros((8, LANES), F32)
        pad[8:8 + s_len, :] = x_ref[...].astype(F32)

    def conv_silu(pad, cw_ref, base):
        win = pad[pl.ds(base, DN_SUPER + 16), :]
        y = cw_ref[0:1, :] * win[6:6 + DN_SUPER, :]
        for j in range(1, DN_CONV):
            y = y + cw_ref[j:j + 1, :] * win[6 + j:6 + j + DN_SUPER, :]
        return y * _sigmoid(y)

    def l2n(y):
        return y * lax.rsqrt(jnp.sum(y * y, axis=-1, keepdims=True) + EPS)

    n_sup = s_len // DN_SUPER
    per_sup = DN_SUPER // CHUNK
    r_i = lax.broadcasted_iota(jnp.int32, (DN_SUPER, DN_SUPER), 0)
    c_i = lax.broadcasted_iota(jnp.int32, (DN_SUPER, DN_SUPER), 1)
    same = lax.shift_right_logical(r_i, CHUNK_SHIFT) == lax.shift_right_logical(c_i, CHUNK_SHIFT)
    incl = (same & (r_i >= c_i), same & (r_i <= c_i))
    strict = (same & (r_i > c_i), same & (r_i < c_i))

    @pl.when(h == 0)
    def _():
        alv = jnp.zeros((1, LANES), F32)
        dtv = jnp.zeros((1, LANES), F32)
        for d in range(2):
            for hh in range(N_HEADS):
                alv = jnp.where(lane == 8 + d * N_HEADS + hh, alog_ref[d, hh], alv)
                dtv = jnp.where(lane == 8 + d * N_HEADS + hh, dtb_ref[d, hh], dtv)
        sm = small_ref[...]
        ball[...] = _sigmoid(sm)
        gcall[...] = -jnp.exp(alv) * _softplus(sm + dtv)
        tri_both = jnp.concatenate([incl[0].astype(BF16), incl[1].astype(BF16)], axis=0)
        fwd_lane = lane < 8 + N_HEADS

        def cumulate(g, carry):
            rows = pl.ds(pl.multiple_of(g * DN_SUPER, DN_SUPER), DN_SUPER)
            both = _sel_left(tri_both, gcall[rows, :])
            gcall[rows, :] = jnp.where(fwd_lane, both[:DN_SUPER], both[DN_SUPER:])
            return carry

        lax.fori_loop(0, n_sup, cumulate, 0)

    def pick(x, idx):
        col = jnp.sum(jnp.where(lane == idx, x, 0.0), axis=-1, keepdims=True)
        return jnp.broadcast_to(col, x.shape)

    for d in range(2):
        al = jnp.full((1, DN_SUPER), alog_ref[d, h], F32)
        dt = jnp.full((1, DN_SUPER), dtb_ref[d, h], F32)
        g_r = -jnp.exp(al) * _softplus(smallt_ref[0, 8 + d * N_HEADS + h] + dt)
        grow[d] = _sel_right(g_r, incl[1 - d].astype(BF16))

    row_blk = lax.shift_right_logical(lax.broadcasted_iota(jnp.int32, (DN_SUPER, 1), 0), CHUNK_SHIFT)

    def stage(g, p, j):
        base = pl.multiple_of(g * DN_SUPER, DN_SUPER)
        rows = pl.ds(base, DN_SUPER)
        stg[p, j, 0] = l2n(conv_silu(qpad, cq_ref, base)) * (HEAD_DIM ** -0.5)
        stg[p, j, 1] = l2n(conv_silu(kpad, ck_ref, base))
        stg[p, j, 2] = conv_silu(vpad, cv_ref, base)
        for d in range(2):
            stg[p, j, 3 + 2 * d] = pick(gcall[rows, :], 8 + d * N_HEADS + h)
            stg[p, j, 4 + 2 * d] = pick(ball[rows, :], d * N_HEADS + h)

    n_pairs = n_sup // DN_PAIR
    chains = [(j, d) for j in range(DN_PAIR) for d in range(2)]

    def prepare_pair(gg, carry):
        p = gg & 1
        nxt = jnp.minimum(gg + 1, n_pairs - 1)
        for j in range(DN_PAIR):
            stage(nxt * DN_PAIR + j, 1 - p, j)
        q = [stg[p, j, 0] for j in range(DN_PAIR)]
        k = [stg[p, j, 1] for j in range(DN_PAIR)]
        v = [stg[p, j, 2] for j in range(DN_PAIR)]
        k16 = [x.astype(BF16) for x in k]
        kk = [_dot_nt(x, x) for x in k16]
        qk_raw = [_dot_nt(x.astype(BF16), y) for x, y in zip(q, k16)]
        gc_c, be_c, decay, a_mats = {}, {}, {}, []
        for j, d in chains:
            gc_c[j, d] = stg[p, j, 3 + 2 * d]
            be_c[j, d] = stg[p, j, 4 + 2 * d]
            gc_r = grow[d, pl.ds(gg * DN_PAIR + j, 1), :]
            diff = jnp.concatenate([gc_c[j, d]] * 2, axis=1) - gc_r
            decay[j, d] = jnp.where(incl[d], jnp.exp(jnp.where(incl[d], diff, 0.0)), 0.0)
            a_mats.append(jnp.where(strict[d], kk[j] * jnp.concatenate([be_c[j, d]] * 2, axis=1) * decay[j, d], 0.0))
        t_inv = _neumann_inverses(a_mats, CHUNK)
        e_gc = {c: jnp.exp(gc_c[c]) for c in chains}
        rhs = [jnp.concatenate([v[j] * be_c[j, d], k[j] * (be_c[j, d] * e_gc[j, d])], axis=1).astype(BF16)
               for j, d in chains]
        uw16 = [_dot(t.astype(BF16), r).astype(BF16) for t, r in zip(t_inv, rhs)]
        qk16 = [(qk_raw[j] * decay[j, d]).astype(BF16) for j, d in chains]
        qu_qw = [_dot(x, y) for x, y in zip(qk16, uw16)]
        gc_last, kd_bd = [], []
        for j, d in chains:
            last = CHUNK - 1 if d == 0 else 0
            gl = [gc_c[j, d][i * CHUNK + last:i * CHUNK + last + 1, :] for i in range(per_sup)]
            gcl = jnp.concatenate([jnp.broadcast_to(x, (CHUNK, LANES)) for x in gl], axis=0)
            kd = k[j] * jnp.exp(gcl - gc_c[j, d])
            gc_last.append(gl)
            kd_bd.append(jnp.concatenate(
                [jnp.where(row_blk == i, kd, 0.0) for i in range(per_sup)], axis=1).astype(BF16))
        mn = [_dot_tn(x, y) for x, y in zip(uw16, kd_bd)]
        for n, (j, d) in enumerate(chains):
            g = gg * DN_PAIR + j
            rows = pl.ds(pl.multiple_of(g * DN_SUPER, DN_SUPER), DN_SUPER)
            for i in range(per_sup):
                c = g * per_sup + i
                cols = slice(i * HEAD_DIM, (i + 1) * HEAD_DIM)
                nt_scr[d, c] = mn[n][:HEAD_DIM, cols]
                mt_scr[d, c] = (-mn[n][HEAD_DIM:, cols]).astype(BF16)
                gl_scr[d, c] = jnp.exp(gc_last[n][i])
            qh_scr[d, rows, :] = (q[j] * e_gc[j, d] - qu_qw[n][:, HEAD_DIM:]).astype(BF16)
            if d == 1:
                of_scr[rows, :] = qu_qw[n - 1][:, :HEAD_DIM] + qu_qw[n][:, :HEAD_DIM]
        return carry

    for j in range(DN_PAIR):
        stage(j, 0, j)
    lax.fori_loop(0, n_pairs, prepare_pair, 0)

    st_scr[...] = jnp.zeros(st_scr.shape, F32)

    def scan_step(i, carry):
        for d in range(2):
            c = i if d == 0 else nc - 1 - i
            rows = pl.ds(pl.multiple_of(c * CHUNK, CHUNK), CHUNK)
            st = st_scr[d]
            st16 = st.astype(BF16)
            of_scr[rows, :] += _dot_nt(qh_scr[d, rows, :], st16)
            st_scr[d] = st * gl_scr[d, c] + _dot(st16, mt_scr[d, c]) + nt_scr[d, c]
        return carry

    lax.fori_loop(0, nc, scan_step, 0)

    o = of_scr[...]
    y = o * lax.rsqrt(jnp.mean(o * o, axis=-1, keepdims=True) + EPS) * nw_ref[...]
    zz = z_ref[...].astype(F32)
    o_ref[...] = (y * (zz * _sigmoid(zz))).astype(o_ref.dtype)


def _deltanet(p_main, small, small_t, conv_w, a_log, dt_bias, norm_w, batch, s_len):
    nc = s_len // CHUNK
    n_sup = s_len // DN_SUPER
    assert s_len % (DN_SUPER * DN_PAIR) == 0
    tok = lambda tile: pl.BlockSpec((s_len, LANES), lambda b, h, tile=tile: (b, tile + h))
    cw =lambda tile: pl.BlockSpec((DN_CONV, LANES), lambda b, h, tile=tile: (0, tile + h))
    smem = pl.BlockSpec(memory_space=pltpu.SMEM)
    return pl.pallas_call(
        _dn_kernel,
        grid=(batch, N_HEADS),
        in_specs=[
            smem, smem,
            tok(T_Q), tok(T_K), tok(T_V), tok(T_Z),
            pl.BlockSpec((s_len, LANES), lambda b, h: (b, 0)),
            pl.BlockSpec((1, N_SMALL, n_sup, DN_SUPER), lambda b, h: (b, 0, 0, 0)),
            cw(0), cw(N_HEADS), cw(2 * N_HEADS),
            pl.BlockSpec((1, LANES), lambda b, h: (0, 0)),
        ],
        out_specs=pl.BlockSpec((s_len, LANES), lambda b, h: (b, h)),
        out_shape=jax.ShapeDtypeStruct((batch * s_len, MIX_WIDTH), ACT_DTYPE),
        scratch_shapes=[
            pltpu.VMEM((s_len + 16, LANES), F32),
            pltpu.VMEM((s_len + 16, LANES), F32),
            pltpu.VMEM((s_len + 16, LANES), F32),
            pltpu.VMEM((2, DN_PAIR, 7, DN_SUPER, LANES), F32),
            pltpu.VMEM((s_len, LANES), F32),
            pltpu.VMEM((s_len, LANES), F32),
            pltpu.VMEM((2, n_sup, DN_SUPER), F32),
            pltpu.VMEM((2, HEAD_DIM, HEAD_DIM), F32),
            pltpu.VMEM((s_len, LANES), F32),
            pltpu.VMEM((2, nc, HEAD_DIM, HEAD_DIM), BF16),
            pltpu.VMEM((2, nc, HEAD_DIM, HEAD_DIM), F32),
            pltpu.VMEM((2, s_len, LANES), BF16),
            pltpu.VMEM((2, nc, 1, LANES), F32),
        ],
        compiler_params=pltpu.CompilerParams(
            dimension_semantics=("arbitrary", "arbitrary"), vmem_limit_bytes=VMEM_LIMIT),
        name="deltanet",
    )(a_log, dt_bias, p_main, p_main, p_main, p_main, small, small_t,
      conv_w, conv_w, conv_w, norm_w)


def _hg_kernel(layer, hq_ref, hff_ref, hfb_ref, hi_ref, hg_ref, lbl_ref, nw_ref, o_ref,
               qs, iv, lf, kk, st_scr, oi_scr, nt_scr, fl_scr, qt_scr):
    s_len = hq_ref.shape[0]
    nc = s_len // CHUNK

    lg = lbl_ref[...].astype(F32)
    e = jnp.exp(lg - jnp.max(lg, axis=0, keepdims=True))
    sm = e / jnp.sum(e, axis=0, keepdims=True)
    lb = jnp.clip(jnp.sum(sm[:layer + 1], axis=0) - sm[0], 0.0, 1.0)

    x = hq_ref[...].astype(F32)
    qs[...] = (x * _sigmoid(x)) * (HEAD_DIM ** -0.5)
    iv[...] = hi_ref[...].astype(BF16)
    for d, ref in enumerate((hff_ref, hfb_ref)):
        fr = ref[...].astype(F32)
        lbd = lb[d:d + 1, :]
        log_sig = jnp.minimum(fr, 0.0) - _log1p_nonneg(jnp.exp(-jnp.abs(fr)))
        lf[d] = log_sig + _log1p_nonneg(lbd * jnp.exp(-fr))
        kk[d] = (1.0 - lbd) * _sigmoid(-fr)

    n_sup = s_len // HG_SUPER
    per_sup = HG_SUPER // CHUNK
    subs = CHUNK // SUB
    r_i = lax.broadcasted_iota(jnp.int32, (HG_SUPER, HG_SUPER), 0)
    c_i = lax.broadcasted_iota(jnp.int32, (HG_SUPER, HG_SUPER), 1)
    same_chunk = lax.shift_right_logical(r_i, CHUNK_SHIFT) == lax.shift_right_logical(c_i, CHUNK_SHIFT)
    r_sub = lax.shift_right_logical(r_i, SUB_SHIFT)
    c_sub = lax.shift_right_logical(c_i, SUB_SHIFT)
    causal = (r_i >= c_i, r_i <= c_i)
    row_i = lax.broadcasted_iota(jnp.int32, (HG_SUPER, 1), 0)
    row_sub = lax.shift_right_logical(row_i, SUB_SHIFT) & (subs - 1)
    row_blk = lax.shift_right_logical(row_i, CHUNK_SHIFT)
    pos16 = lax.broadcasted_iota(jnp.int32, (1, SUB, 1), 1)

    def group_row(x, group, idx):
        x3 = x.reshape(x.shape[0] // group, group, LANES)
        return jnp.broadcast_to(x3[:, idx:idx + 1, :], x3.shape).reshape(x.shape)

    chains = [(j, d) for j in range(HG_PAIR) for d in range(2)]
    tri = [(same_chunk & causal[d]).astype(BF16) for d in range(2)]
    diag_mask = [(r_sub == c_sub) & causal[d] for d in range(2)]

    def prepare_pair(gg, carry):
        sup = [gg * HG_PAIR + j for j in range(HG_PAIR)]
        rows = [pl.ds(pl.multiple_of(g * HG_SUPER, HG_SUPER), HG_SUPER) for g in sup]
        q = [qs[r, :] for r in rows]
        i16 = [iv[r, :] for r in rows]
        k = {(j, d): kk[d, rows[j], :] for j, d in chains}
        b = {(j, d): _sel_left(tri[d], lf[d, rows[j], :]) for j, d in chains}
        qh, keys, masks, expo = {}, {}, {}, {}
        for c in chains:
            j, d = c
            first = 0 if d == 0 else SUB - 1
            ref16 = group_row(b[c], SUB, first)
            expo[c] = ref16 - b[c]
            qh[c] = (q[j] * jnp.exp(b[c] - ref16)).astype(BF16)
            keys[c] = [(k[c] * jnp.exp(jnp.minimum(expo[c], HG_SAFE_EXP))).astype(BF16)]
            masks[c] = []
            for t_sub in (range(1, subs) if d == 0 else range(subs - 1)):
                ref_t = group_row(b[c], CHUNK, t_sub * SUB + first)
                src = (row_sub < t_sub) if d == 0 else (row_sub > t_sub)
                keys[c].append(
                    jnp.where(src, k[c] * jnp.exp(jnp.where(src, ref_t - b[c], 0.0)), 0.0).astype(BF16))
                masks[c].append(same_chunk & ((r_sub & (subs - 1)) == t_sub) & (
                    ((c_sub & (subs - 1)) < t_sub) if d == 0 else ((c_sub & (subs - 1)) > t_sub)))
        att_all = {c: _dot_nt(qh[c], jnp.concatenate(keys[c], axis=0)) for c in chains}
        att_off, att16 = {}, {}
        for c in chains:
            off = jnp.zeros((HG_SUPER, HG_SUPER), F32)
            for n, m in enumerate(masks[c]):
                off = off + jnp.where(m, att_all[c][:, (n + 1) * HG_SUPER:(n + 2) * HG_SUPER], 0.0)
            att_off[c] = off
            att16[c] = (off + jnp.where(diag_mask[c[1]], att_all[c][:, :HG_SUPER], 0.0)).astype(BF16)
        o_fast = {c: _dot(att16[c], i16[c[0]]) for c in chains}
        b_last, kt_bd = {}, {}
        for c in chains:
            j, d = c
            oi_scr[d, rows[j], :] = o_fast[c]
            b_last[c] = group_row(b[c], CHUNK, CHUNK - 1 if d == 0 else 0)
            qt_scr[d, rows[j], :] = (q[j] * jnp.exp(b[c])).astype(BF16)
            kt = k[c] * jnp.exp(b_last[c] - b[c])
            kt_bd[c] = jnp.concatenate(
                [jnp.where(row_blk == n, kt, 0.0) for n in range(per_sup)], axis=1).astype(BF16)
        nts = {c: _dot_tn(i16[c[0]], kt_bd[c]) for c in chains}
        for c in chains:
            j, d = c
            for n in range(per_sup):
                cc = sup[j] * per_sup + n
                nt_scr[d, cc] = nts[c][:, n * HEAD_DIM:(n + 1) * HEAD_DIM]
                fl_scr[d, cc] = jnp.exp(b_last[c][n * CHUNK:n * CHUNK + 1, :])

        for c in chains:
            j, d = c

            @pl.when(jnp.max(expo[c]) > HG_SAFE_EXP)
            def _(c=c, j=j, d=d):
                b3 = b[c].reshape(HG_SUPER // SUB, SUB, LANES)
                q3 = q[j].reshape(b3.shape)
                k3 = k[c].reshape(b3.shape)
                i3 = i16[j].astype(F32).reshape(b3.shape)
                acc = jnp.zeros(b3.shape, F32)
                for s in range(SUB):
                    ok = (pos16 >= s) if d == 0 else (pos16 <= s)
                    ex = jnp.exp(jnp.where(ok, b3 - b3[:, s:s + 1, :], 0.0))
                    col = jnp.sum(q3 * k3[:, s:s + 1, :] * ex, axis=-1, keepdims=True)
                    acc = acc + jnp.where(ok, col, 0.0) * i3[:, s:s + 1, :]
                oi_scr[d, rows[j], :] = (_dot(att_off[c].astype(BF16), i16[j])
                                         + acc.reshape(HG_SUPER, LANES))
        return carry

    lax.fori_loop(0, n_sup // HG_PAIR, prepare_pair, 0)

    st_scr[...] = jnp.zeros(st_scr.shape, F32)

    def scan_step(ii, carry):
        for u in range(HG_SCAN_UNROLL):
            i = ii * HG_SCAN_UNROLL + u
            for d in range(2):
                c = i if d == 0 else nc - 1 - i
                rows = pl.ds(pl.multiple_of(c * CHUNK, CHUNK), CHUNK)
                st = st_scr[d]
                oi_scr[d, rows, :] += _dot_nt(qt_scr[d, rows, :], st.astype(BF16))
                st_scr[d] = st * fl_scr[d, c] + nt_scr[d, c]
        return carry

    lax.fori_loop(0, nc // HG_SCAN_UNROLL, scan_step, 0)

    o = oi_scr[0] + oi_scr[1]
    y = o * lax.rsqrt(jnp.mean(o * o, axis=-1, keepdims=True) + EPS) * nw_ref[...]
    zz = hg_ref[...].astype(F32)
    o_ref[...] = (y * (zz * _sigmoid(zz))).astype(o_ref.dtype)


def _hgrn2(p_main, lb_logits, norm_w, layer, batch, s_len):
    nc = s_len // CHUNK
    assert s_len % HG_SUPER == 0
    depth = lb_logits.shape[0]
    tok = lambda tile: pl.BlockSpec((s_len, LANES), lambda b, h, tile=tile: (b, tile + h))
    return pl.pallas_call(
        functools.partial(_hg_kernel, layer),
        grid=(batch, N_HEADS),
        in_specs=[
            tok(T_HQ), tok(T_HFF), tok(T_HFB), tok(T_HI), tok(T_HGATE),
            pl.BlockSpec((depth, 2, LANES), lambda b, h: (0, 0, h)),
            pl.BlockSpec((1, LANES), lambda b, h: (0, 0)),
        ],
        out_specs=pl.BlockSpec((s_len, LANES), lambda b, h: (b, h)),
        out_shape=jax.ShapeDtypeStruct((batch * s_len, MIX_WIDTH), ACT_DTYPE),
        scratch_shapes=[
            pltpu.VMEM((s_len, LANES), F32),
            pltpu.VMEM((s_len, LANES), BF16),
            pltpu.VMEM((2, s_len, LANES), F32),
            pltpu.VMEM((2, s_len, LANES), F32),
            pltpu.VMEM((2, HEAD_DIM, HEAD_DIM), F32),
            pltpu.VMEM((2, s_len, LANES), F32),
            pltpu.VMEM((2, nc, HEAD_DIM, HEAD_DIM), F32),
            pltpu.VMEM((2, nc, 1, LANES), F32),
            pltpu.VMEM((2, s_len, LANES), BF16),
        ],
        compiler_params=pltpu.CompilerParams(
            dimension_semantics=("arbitrary", "arbitrary"), vmem_limit_bytes=VMEM_LIMIT),
        name="hgrn2",
    )(p_main, p_main, p_main, p_main, p_main, lb_logits, norm_w)


def _merge_kernel(x_ref, odn_ref, ohg_ref, gdn_ref, ghg_ref, wdn_ref, whg_ref, wo_ref, o_ref):
    a = _dot(odn_ref[...].astype(BF16), wdn_ref[...])
    b = _dot(ohg_ref[...].astype(BF16), whg_ref[...])
    merged = _sigmoid(gdn_ref[...].astype(F32)) * a + _sigmoid(ghg_ref[...].astype(F32)) * b
    o_ref[...] = x_ref[...] + _dot(merged.astype(BF16), wo_ref[...])


def _merge_out(x2d, o_dn, o_hg, p_main, w_dn, w_hg, w_out):
    t = x2d.shape[0]
    tm = min(1024, t)
    const = lambda shape: pl.BlockSpec(shape, lambda i: (0, 0))
    return pl.pallas_call(
        _merge_kernel,
        grid=(t // tm,),
        in_specs=[
            pl.BlockSpec((tm, D_MODEL), lambda i: (i, 0)),
            pl.BlockSpec((tm, MIX_WIDTH), lambda i: (i, 0)),
            pl.BlockSpec((tm, MIX_WIDTH), lambda i: (i, 0)),
            pl.BlockSpec((tm, D_MODEL), lambda i: (i, T_GATE_DN)),
            pl.BlockSpec((tm, D_MODEL), lambda i: (i, T_GATE_HG)),
            const((MIX_WIDTH, D_MODEL)), const((MIX_WIDTH, D_MODEL)), const((D_MODEL, D_MODEL)),
        ],
        out_specs=pl.BlockSpec((tm, D_MODEL), lambda i: (i, 0)),
        out_shape=jax.ShapeDtypeStruct((t, D_MODEL), F32),
        compiler_params=pltpu.CompilerParams(
            dimension_semantics=("arbitrary",), vmem_limit_bytes=VMEM_LIMIT),
        name="merge_out",
    )(x2d, o_dn, o_hg, p_main, p_main, w_dn, w_hg, w_out)


FF_COLS = D_FF // 2


def _ffn_kernel(final, xm_ref, xp_ref, xn_ref, nw_ref, wg_ref, wu_ref, cw_ref, cb_ref, wd_ref, fn_ref,
                o_ref, g_scr):
    i = pl.program_id(1)
    last = pl.num_programs(1) - 1
    ts = xm_ref.shape[1]
    x = xm_ref[0]
    h_ext = _rms(jnp.concatenate([xp_ref[0], x, xn_ref[0]], axis=0), nw_ref[...])
    h = h_ext[8:8 + ts].astype(BF16)
    h_ext = h_ext.astype(BF16)
    acc = x
    for c in range(D_FF // FF_COLS):
        cols = slice(c * FF_COLS, (c + 1) * FF_COLS)
        g_scr[...] = _dot(h_ext, wg_ref[:, cols])
        g_scr[7:8, :] = jnp.where(i > 0, g_scr[7:8, :], 0.0)
        g_scr[8 + ts:9 + ts, :] = jnp.where(i < last, g_scr[8 + ts:9 + ts, :], 0.0)
        gc = (cw_ref[0:1, cols] * g_scr[7:7 + ts, :] + cw_ref[1:2, cols] * g_scr[8:8 + ts, :]
              + cw_ref[2:3, cols] * g_scr[9:9 + ts, :] + cb_ref[:, cols])
        u = _dot(h, wu_ref[:, cols])
        act = (gc * _sigmoid(gc)) * u
        acc = acc + _dot(act.astype(BF16), wd_ref[cols, :])
    if final:
        acc = _rms(acc, fn_ref[...])
    o_ref[0] = acc


def _ffn(x3d, norm_w, w_gate, w_upp, conv_w, conv_b, w_down, final_w, final):
    batch, s_len, _ = x3d.shape
    ts = min(512, s_len)
    r8 = ts // 8
    n8 = s_len // 8
    const = lambda shape: pl.BlockSpec(shape, lambda b, i: (0, 0), pipeline_mode=pl.Buffered(1))
    return pl.pallas_call(
        functools.partial(_ffn_kernel, final),
        grid=(batch, s_len // ts),
        in_specs=[
            pl.BlockSpec((1, ts, D_MODEL), lambda b, i: (b, i, 0)),
            pl.BlockSpec((1, 8, D_MODEL), lambda b, i: (b, jnp.maximum(i * r8 - 1, 0), 0)),
            pl.BlockSpec((1, 8, D_MODEL), lambda b, i: (b, jnp.minimum((i + 1) * r8, n8 - 1), 0)),
            const((1, D_MODEL)),
            const((D_MODEL, D_FF)), const((D_MODEL, D_FF)),
            const((3, D_FF)), const((1, D_FF)),
            const((D_FF, D_MODEL)),
            const((1, D_MODEL)),
        ],
        out_specs=pl.BlockSpec((1, ts, D_MODEL), lambda b, i: (b, i, 0)),
        out_shape=jax.ShapeDtypeStruct(x3d.shape, F32),
        scratch_shapes=[pltpu.VMEM((ts + 16, FF_COLS), F32)],
        compiler_params=pltpu.CompilerParams(
            dimension_semantics=("arbitrary", "arbitrary"), vmem_limit_bytes=VMEM_LIMIT),
        name="ffn",
    )(x3d, x3d, x3d, norm_w, w_gate, w_upp, conv_w, conv_b, w_down, final_w)


def kernel(x, mix_norm, w_in, dn_conv, dn_a_log, dn_dt_bias, dn_norm, hg_lb_logits, hg_norm,
           w_branch_dn, w_branch_hg, w_out, ffn_norm, w_up, ffn_conv, ffn_conv_bias, w_down, final_norm):
    batch, s_len, _ = x.shape
    depth = mix_norm.shape[0]
    nc = s_len // CHUNK
    x2d = x.reshape(batch * s_len, D_MODEL)
    final_w = final_norm.reshape(1, D_MODEL)
    for l in range(depth):
        w = w_in[l]
        w_main = jnp.concatenate(
            [w[:, _C_GATE:], w[:, :_C_Z_END], w[:, _C_HQ:_C_GATE]], axis=1).astype(BF16)
        w_s = w[:, _C_Z_END:_C_HQ].astype(BF16)
        w_small = jnp.pad(w_s, ((0, 0), (0, LANES - N_SMALL)))
        p_main, small, small_t = _inproj(x2d, mix_norm[l].reshape(1, D_MODEL), w_main, w_small, w_s.T)
        small_t = small_t.reshape(N_SMALL, batch, s_len // DN_SUPER, DN_SUPER).transpose(1, 0, 2, 3)
        o_dn = _deltanet(p_main, small, small_t, dn_conv[l], dn_a_log[l], dn_dt_bias[l],
                         dn_norm[l].reshape(1, HEAD_DIM), batch, s_len)
        o_hg = _hgrn2(p_main, hg_lb_logits, hg_norm[l].reshape(1, HEAD_DIM), l, batch, s_len)
        x2d = _merge_out(x2d, o_dn, o_hg, p_main, w_branch_dn[l].astype(BF16),
                         w_branch_hg[l].astype(BF16), w_out[l].astype(BF16))
        wu = w_up[l].astype(BF16)
        x3d = _ffn(x2d.reshape(batch, s_len, D_MODEL), ffn_norm[l].reshape(1, D_MODEL),
                   wu[:, :D_FF], wu[:, D_FF:], ffn_conv[l], ffn_conv_bias[l].reshape(1, D_FF),
                   w_down[l].astype(BF16), final_w, l == depth - 1)
        x2d = x3d.reshape(batch * s_len, D_MODEL)
    return x2d.reshape(batch, s_len, D_MODEL)
```

```python
import functools

import jax
import jax.numpy as jnp
from jax import lax
from jax.experimental import pallas as pl
from jax.experimental.pallas import tpu as pltpu

F32 = jnp.float32
BF16 = jnp.bfloat16

LANES = 128
D_MODEL = 1024
N_HEADS = 4
HEAD_DIM = 128
MIX_WIDTH = N_HEADS * HEAD_DIM
DN_CONV = 5
D_FF = 2816
EPS = 1e-6
CHUNK = 64
SUB = 16
SUB_SHIFT = SUB.bit_length() - 1
HG_SUPER = 256
HG_PAIR = 2
HG_SCAN_UNROLL = 4
HG_SAFE_EXP = 60.0
CHUNK_SHIFT = CHUNK.bit_length() - 1
DN_SUPER = 256
DN_PAIR = 2
DN_SEQS = 2
ACT_DTYPE = BF16
VMEM_LIMIT = 56 * 1024 * 1024

N_MAIN = 2 * D_MODEL + 4 * MIX_WIDTH + 5 * MIX_WIDTH
T_GATE_DN, T_GATE_HG = 0, 1
T_Q, T_K, T_V, T_Z = 16, 20, 24, 28
T_HQ, T_HFF, T_HFB, T_HI, T_HGATE = 32, 36, 40, 44, 48
N_SMALL = 16

_C_Z_END = 4 * MIX_WIDTH
_C_HQ = _C_Z_END + N_SMALL
_C_GATE = _C_HQ + 5 * MIX_WIDTH


def _sigmoid(x):
    return 0.5 * jnp.tanh(0.5 * x) + 0.5


def _log1p_nonneg(x):
    return jnp.log(1.0 + x)


def _softplus(x):
    return jnp.maximum(x, 0.0) + _log1p_nonneg(jnp.exp(-jnp.abs(x)))


def _rms(x, w):
    return x * lax.rsqrt(jnp.mean(x * x, axis=-1, keepdims=True) + EPS) * w


def _dot(a, b):
    return jnp.dot(a, b, preferred_element_type=F32)


def _dot_nt(a, b):
    return lax.dot_general(a, b, (((1,), (1,)), ((), ())), preferred_element_type=F32)


def _dot_tn(a, b):
    return lax.dot_general(a, b, (((0,), (0,)), ((), ())), preferred_element_type=F32)


def _split3(x):
    hi = x.astype(BF16)
    r = x - hi.astype(F32)
    mid = r.astype(BF16)
    lo = (r - mid.astype(F32)).astype(BF16)
    return hi, mid, lo


def _sel_left(m16, x):
    w = x.shape[1]
    y = _dot(m16, jnp.concatenate(_split3(x), axis=1))
    return y[:, :w] + y[:, w:2 * w] + y[:, 2 * w:]


def _sel_right(x, m16):
    r = x.shape[0]
    y = _dot(jnp.concatenate(_split3(x), axis=0), m16)
    return y[:r] + y[r:2 * r] + y[2 * r:]


def _tri(n, lower, strict=False):
    r = lax.broadcasted_iota(jnp.int32, (n, n), 0)
    c = lax.broadcasted_iota(jnp.int32, (n, n), 1)
    if lower:
        return (r > c) if strict else (r >= c)
    return (r < c) if strict else (r <= c)


def _inproj_kernel(x_ref, nw_ref, w_ref, ws_ref, wst_ref, p_ref, s_ref, st_ref, h_ref):
    @pl.when(pl.program_id(1) == 0)
    def _():
        hb = _rms(x_ref[...], nw_ref[...]).astype(BF16)
        h_ref[...] = hb
        s_ref[...] = _dot(hb, ws_ref[...])
        st_ref[...] = _dot_nt(wst_ref[...], hb)

    p_ref[...] = _dot(h_ref[...], w_ref[...]).astype(p_ref.dtype)


def _inproj(x2d, norm_w, w_main, w_small, w_small_t):
    t = x2d.shape[0]
    tm = min(1024, t)
    tn = N_MAIN // 2
    return pl.pallas_call(
        _inproj_kernel,
        grid=(t // tm, N_MAIN // tn),
        in_specs=[
            pl.BlockSpec((tm, D_MODEL), lambda i, j: (i, 0)),
            pl.BlockSpec((1, D_MODEL), lambda i, j: (0, 0)),
            pl.BlockSpec((D_MODEL, tn), lambda i, j: (0, j)),
            pl.BlockSpec((D_MODEL, LANES), lambda i, j: (0, 0)),
            pl.BlockSpec((N_SMALL, D_MODEL), lambda i, j: (0, 0)),
        ],
        out_specs=[
            pl.BlockSpec((tm, tn), lambda i, j: (i, j)),
            pl.BlockSpec((tm, LANES), lambda i, j: (i, 0)),
            pl.BlockSpec((N_SMALL, tm), lambda i, j: (0, i)),
        ],
        out_shape=[
            jax.ShapeDtypeStruct((t, N_MAIN), ACT_DTYPE),
            jax.ShapeDtypeStruct((t, LANES), F32),
            jax.ShapeDtypeStruct((N_SMALL, t), F32),
        ],
        scratch_shapes=[pltpu.VMEM((tm, D_MODEL), BF16)],
        compiler_params=pltpu.CompilerParams(
            dimension_semantics=("arbitrary", "arbitrary"), vmem_limit_bytes=VMEM_LIMIT),
        name="inproj",
    )(x2d, norm_w, w_main, w_small, w_small_t)


def _neumann_inverses(mats, block):
    n = mats[0].shape[0]
    nb = n // block
    shift = block.bit_length() - 1
    steps = shift
    r_p = lax.broadcasted_iota(jnp.int32, (block, n), 0)
    c_p = lax.broadcasted_iota(jnp.int32, (block, n), 1)
    c_blk = lax.shift_right_logical(c_p, shift)
    r_n = lax.broadcasted_iota(jnp.int32, (n, n), 0)
    c_n = lax.broadcasted_iota(jnp.int32, (n, n), 1)
    same = lax.shift_right_logical(r_n, shift) == lax.shift_right_logical(c_n, shift)

    def pack(m):
        out = m[0:block]
        for j in range(1, nb):
            out = jnp.where(c_blk == j, m[j * block:(j + 1) * block], out)
        return out

    def unpack(p):
        return jnp.where(same, jnp.concatenate([p] * nb, axis=0), 0.0)

    eye_p = (r_p == (c_p & (block - 1))).astype(F32)
    xs = [pack(a) for a in mats]
    ss = [eye_p - x for x in xs]
    xs = [_dot(x.astype(BF16), a.astype(BF16)) for x, a in zip(xs, mats)]
    for k in range(1, steps):
        x_bd = [unpack(x).astype(BF16) for x in xs]
        if k < steps - 1:
            lhs = [jnp.concatenate([s, x], axis=0).astype(BF16) for s, x in zip(ss, xs)]
            both = [_dot(l, xb) for l, xb in zip(lhs, x_bd)]
            ss = [s + b[:block] for s, b in zip(ss, both)]
            xs = [b[block:] for b in both]
        else:
            s16 = [s.astype(BF16) for s in ss]
            ss = [s + _dot(l, xb) for s, l, xb in zip(ss, s16, x_bd)]
    return [unpack(s) for s in ss]


def _dn_kernel(seqs, alog_ref, dtb_ref, q_ref, k_ref, v_ref, z_ref, small_ref, smallt_ref,
               cq_ref, ck_ref, cv_ref, nw_ref, o_ref,
               qpad, kpad, vpad, stg, ball, gcall, grow, st_scr, of_scr, mt_scr, nt_scr, qh_scr,
               gl_scr):
    h = pl.program_id(1)
    s_len = q_ref.shape[0] // seqs
    nc = s_len // CHUNK
    lane = lax.broadcasted_iota(jnp.int32, (1, LANES), 1)

    for pad, x_ref in ((qpad, q_ref), (kpad, k_ref), (vpad, v_ref)):
        for s in range(seqs):
            pad[s, 0:8, :] = jnp.zeros((8, LANES), F32)
            pad[s, 8 + s_len:16 + s_len, :] = jnp.zeros((8, LANES), F32)
            pad[s, 8:8 + s_len, :] = x_ref[s * s_len:(s + 1) * s_len, :].astype(F32)

    def conv_silu(pad, cw_ref, seq, base):
        win = pad[seq, pl.ds(base, DN_SUPER + 16), :]
        y = cw_ref[0:1, :] * win[6:6 + DN_SUPER, :]
        for j in range(1, DN_CONV):
            y = y + cw_ref[j:j + 1, :] * win[6 + j:6 + j + DN_SUPER, :]
        return y * _sigmoid(y)

    def l2n(y):
        return y * lax.rsqrt(jnp.sum(y * y, axis=-1, keepdims=True) + EPS)

    n_sup = s_len // DN_SUPER
    n_tot = seqs * n_sup
    per_sup = DN_SUPER // CHUNK
    r_i = lax.broadcasted_iota(jnp.int32, (DN_SUPER, DN_SUPER), 0)
    c_i = lax.broadcasted_iota(jnp.int32, (DN_SUPER, DN_SUPER), 1)
    same = lax.shift_right_logical(r_i, CHUNK_SHIFT) == lax.shift_right_logical(c_i, CHUNK_SHIFT)
    incl = (same & (r_i >= c_i), same & (r_i <= c_i))
    strict = (same & (r_i > c_i), same & (r_i < c_i))

    @pl.when(h == 0)
    def _():
        alv = jnp.zeros((1, LANES), F32)
        dtv = jnp.zeros((1, LANES), F32)
        for d in range(2):
            for hh in range(N_HEADS):
                alv = jnp.where(lane == 8 + d * N_HEADS + hh, alog_ref[d, hh], alv)
                dtv = jnp.where(lane == 8 + d * N_HEADS + hh, dtb_ref[d, hh], dtv)
        sm = small_ref[...]
        ball[...] = _sigmoid(sm)
        gcall[...] = -jnp.exp(alv) * _softplus(sm + dtv)
        tri_both = jnp.concatenate([incl[0].astype(BF16), incl[1].astype(BF16)], axis=0)
        fwd_lane = lane < 8 + N_HEADS

        def cumulate(g, carry):
            rows = pl.ds(pl.multiple_of(g * DN_SUPER, DN_SUPER), DN_SUPER)
            both = _sel_left(tri_both, gcall[rows, :])
            gcall[rows, :] = jnp.where(fwd_lane, both[:DN_SUPER], both[DN_SUPER:])
            return carry

        lax.fori_loop(0, n_tot, cumulate, 0)

    def pick(x, idx):
        col = jnp.sum(jnp.where(lane == idx, x, 0.0), axis=-1, keepdims=True)
        return jnp.broadcast_to(col, x.shape)

    for d in range(2):
        al = jnp.full((1, DN_SUPER), alog_ref[d, h], F32)
        dt = jnp.full((1, DN_SUPER), dtb_ref[d, h], F32)
        for s in range(seqs):
            g_r = -jnp.exp(al) * _softplus(smallt_ref[s, 8 + d * N_HEADS + h] + dt)
            grow[d, s * n_sup:(s + 1) * n_sup, :] = _sel_right(g_r, incl[1 - d].astype(BF16))

    row_blk = lax.shift_right_logical(lax.broadcasted_iota(jnp.int32, (DN_SUPER, 1), 0), CHUNK_SHIFT)

    def stage(g, p, j):
        seq = lax.div(g, n_sup)
        base = pl.multiple_of(lax.rem(g, n_sup) * DN_SUPER, DN_SUPER)
        rows = pl.ds(pl.multiple_of(g * DN_SUPER, DN_SUPER), DN_SUPER)
        stg[p, j, 0] = l2n(conv_silu(qpad, cq_ref, seq, base)) * (HEAD_DIM ** -0.5)
        stg[p, j, 1] = l2n(conv_silu(kpad, ck_ref, seq, base))
        stg[p, j, 2] = conv_silu(vpad, cv_ref, seq, base)
        for d in range(2):
            stg[p, j, 3 + 2 * d] = pick(gcall[rows, :], 8 + d * N_HEADS + h)
            stg[p, j, 4 + 2 * d] = pick(ball[rows, :], d * N_HEADS + h)

    n_pairs = n_tot // DN_PAIR
    chains = [(j, d) for j in range(DN_PAIR) for d in range(2)]

    def prepare_pair(gg, carry):
        p = gg & 1
        nxt = jnp.minimum(gg + 1, n_pairs - 1)
        for j in range(DN_PAIR):
            stage(nxt * DN_PAIR + j, 1 - p, j)
        q = [stg[p, j, 0] for j in range(DN_PAIR)]
        k = [stg[p, j, 1] for j in range(DN_PAIR)]
        v = [stg[p, j, 2] for j in range(DN_PAIR)]
        k16 = [x.astype(BF16) for x in k]
        kk = [_dot_nt(x, x) for x in k16]
        qk_raw = [_dot_nt(x.astype(BF16), y) for x, y in zip(q, k16)]
        gc_c, be_c, decay, a_mats = {}, {}, {}, []
        for j, d in chains:
            gc_c[j, d] = stg[p, j, 3 + 2 * d]
            be_c[j, d] = stg[p, j, 4 + 2 * d]
            gc_r = grow[d, pl.ds(gg * DN_PAIR + j, 1), :]
            diff = jnp.concatenate([gc_c[j, d]] * 2, axis=1) - gc_r
            decay[j, d] = jnp.where(incl[d], jnp.exp(jnp.where(incl[d], diff, 0.0)), 0.0)
            a_mats.append(jnp.where(strict[d], kk[j] * jnp.concatenate([be_c[j, d]] * 2, axis=1) * decay[j, d], 0.0))
        t_inv = _neumann_inverses(a_mats, CHUNK)
        e_gc = {c: jnp.exp(gc_c[c]) for c in chains}
        rhs = [jnp.concatenate([v[j] * be_c[j, d], k[j] * (be_c[j, d] * e_gc[j, d])], axis=1).astype(BF16)
               for j, d in chains]
        uw16 = [_dot(t.astype(BF16), r).astype(BF16) for t, r in zip(t_inv, rhs)]
        qk16 = [(qk_raw[j] * decay[j, d]).astype(BF16) for j, d in chains]
        qu_qw = [_dot(x, y) for x, y in zip(qk16, uw16)]
        gc_last, kd_bd = [], []
        for j, d in chains:
            last = CHUNK - 1 if d == 0 else 0
            gl = [gc_c[j, d][i * CHUNK + last:i * CHUNK + last + 1, :] for i in range(per_sup)]
            gcl = jnp.concatenate([jnp.broadcast_to(x, (CHUNK, LANES)) for x in gl], axis=0)
            kd = k[j] * jnp.exp(gcl - gc_c[j, d])
            gc_last.append(gl)
            kd_bd.append(jnp.concatenate(
                [jnp.where(row_blk == i, kd, 0.0) for i in range(per_sup)], axis=1).astype(BF16))
        mn = [_dot_tn(x, y) for x, y in zip(uw16, kd_bd)]
        for n, (j, d) in enumerate(chains):
            g = gg * DN_PAIR + j
            rows = pl.ds(pl.multiple_of(g * DN_SUPER, DN_SUPER), DN_SUPER)
            for i in range(per_sup):
                c = g * per_sup + i
                cols = slice(i * HEAD_DIM, (i + 1) * HEAD_DIM)
                nt_scr[d, c] = mn[n][:HEAD_DIM, cols]
                mt_scr[d, c] = (-mn[n][HEAD_DIM:, cols]).astype(BF16)
                gl_scr[d, c] = jnp.exp(gc_last[n][i])
            qh_scr[d, rows, :] = (q[j] * e_gc[j, d] - qu_qw[n][:, HEAD_DIM:]).astype(BF16)
            if d == 1:
                of_scr[rows, :] = qu_qw[n - 1][:, :HEAD_DIM] + qu_qw[n][:, :HEAD_DIM]
        return carry

    for j in range(DN_PAIR):
        stage(j, 0, j)
    lax.fori_loop(0, n_pairs, prepare_pair, 0)

    st_scr[...] = jnp.zeros(st_scr.shape, F32)

    def scan_step(i, carry):
        for s in range(seqs):
            for d in range(2):
                c = s * nc + (i if d == 0 else nc - 1 - i)
                rows = pl.ds(pl.multiple_of(c * CHUNK, CHUNK), CHUNK)
                st = st_scr[s, d]
                st16 = st.astype(BF16)
                of_scr[rows, :] += _dot_nt(qh_scr[d, rows, :], st16)
                st_scr[s, d] = st * gl_scr[d, c] + _dot(st16, mt_scr[d, c]) + nt_scr[d, c]
        return carry

    lax.fori_loop(0, nc, scan_step, 0)

    o = of_scr[...]
    y = o * lax.rsqrt(jnp.mean(o * o, axis=-1, keepdims=True) + EPS) * nw_ref[...]
    zz = z_ref[...].astype(F32)
    o_ref[...] = (y * (zz * _sigmoid(zz))).astype(o_ref.dtype)


def _deltanet(p_main, small, small_t, conv_w, a_log, dt_bias, norm_w, batch, s_len):
    seqs = DN_SEQS if batch % DN_SEQS == 0 else 1
    n_rows = seqs * s_len
    nc = n_rows // CHUNK
    n_sup = s_len // DN_SUPER
    assert s_len % DN_SUPER == 0 and (seqs * n_sup) % DN_PAIR == 0
    tok = lambda tile: pl.BlockSpec((n_rows, LANES), lambda b, h, tile=tile: (b, tile + h))
    cw = lambda tile: pl.BlockSpec((DN_CONV, LANES), lambda b, h, tile=tile: (0, tile + h))
    smem = pl.BlockSpec(memory_space=pltpu.SMEM)
    return pl.pallas_call(
        functools.partial(_dn_kernel, seqs),
        grid=(batch // seqs, N_HEADS),
        in_specs=[
            smem, smem,
            tok(T_Q), tok(T_K), tok(T_V), tok(T_Z),
            pl.BlockSpec((n_rows, LANES), lambda b, h: (b, 0)),
            pl.BlockSpec((seqs, N_SMALL, n_sup, DN_SUPER), lambda b, h: (b, 0, 0, 0)),
            cw(0), cw(N_HEADS), cw(2 * N_HEADS),
            pl.BlockSpec((1, LANES), lambda b, h: (0, 0)),
        ],
        out_specs=pl.BlockSpec((n_rows, LANES), lambda b, h: (b, h)),
        out_shape=jax.ShapeDtypeStruct((batch * s_len, MIX_WIDTH), ACT_DTYPE),
        scratch_shapes=[
            pltpu.VMEM((seqs, s_len + 16, LANES), F32),
            pltpu.VMEM((seqs, s_len + 16, LANES), F32),
            pltpu.VMEM((seqs, s_len + 16, LANES), F32),
            pltpu.VMEM((2, DN_PAIR, 7, DN_SUPER, LANES), F32),
            pltpu.VMEM((n_rows, LANES), F32),
            pltpu.VMEM((n_rows, LANES), F32),
            pltpu.VMEM((2, seqs * n_sup, DN_SUPER), F32),
            pltpu.VMEM((seqs, 2, HEAD_DIM, HEAD_DIM), F32),
            pltpu.VMEM((n_rows, LANES), F32),
            pltpu.VMEM((2, nc, HEAD_DIM, HEAD_DIM), BF16),
            pltpu.VMEM((2, nc, HEAD_DIM, HEAD_DIM), F32),
            pltpu.VMEM((2, n_rows, LANES), BF16),
            pltpu.VMEM((2, nc, 1, LANES), F32),
        ],
        compiler_params=pltpu.CompilerParams(
            dimension_semantics=("arbitrary", "arbitrary"), vmem_limit_bytes=VMEM_LIMIT),
        name="deltanet",
    )(a_log, dt_bias, p_main, p_main, p_main, p_main, small, small_t,
      conv_w, conv_w, conv_w, norm_w)


def _hg_kernel(layer, hq_ref, hff_ref, hfb_ref, hi_ref, hg_ref, lbl_ref, nw_ref, o_ref,
               qs, iv, lf, kk, st_scr, oi_scr, nt_scr, fl_scr, qt_scr):
    s_len = hq_ref.shape[0]
    nc = s_len // CHUNK

    lg = lbl_ref[...].astype(F32)
    e = jnp.exp(lg - jnp.max(lg, axis=0, keepdims=True))
    sm = e / jnp.sum(e, axis=0, keepdims=True)
    lb = jnp.clip(jnp.sum(sm[:layer + 1], axis=0) - sm[0], 0.0, 1.0)

    x = hq_ref[...].astype(F32)
    qs[...] = (x * _sigmoid(x)) * (HEAD_DIM ** -0.5)
    iv[...] = hi_ref[...].astype(BF16)
    for d, ref in enumerate((hff_ref, hfb_ref)):
        fr = ref[...].astype(F32)
        lbd = lb[d:d + 1, :]
        log_sig = jnp.minimum(fr, 0.0) - _log1p_nonneg(jnp.exp(-jnp.abs(fr)))
        lf[d] = log_sig + _log1p_nonneg(lbd * jnp.exp(-fr))
        kk[d] = (1.0 - lbd) * _sigmoid(-fr)

    n_sup = s_len // HG_SUPER
    per_sup = HG_SUPER // CHUNK
    subs = CHUNK // SUB
    r_i = lax.broadcasted_iota(jnp.int32, (HG_SUPER, HG_SUPER), 0)
    c_i = lax.broadcasted_iota(jnp.int32, (HG_SUPER, HG_SUPER), 1)
    same_chunk = lax.shift_right_logical(r_i, CHUNK_SHIFT) == lax.shift_right_logical(c_i, CHUNK_SHIFT)
    r_sub = lax.shift_right_logical(r_i, SUB_SHIFT)
    c_sub = lax.shift_right_logical(c_i, SUB_SHIFT)
    causal = (r_i >= c_i, r_i <= c_i)
    row_i = lax.broadcasted_iota(jnp.int32, (HG_SUPER, 1), 0)
    row_sub = lax.shift_right_logical(row_i, SUB_SHIFT) & (subs - 1)
    row_blk = lax.shift_right_logical(row_i, CHUNK_SHIFT)
    pos16 = lax.broadcasted_iota(jnp.int32, (1, SUB, 1), 1)

    def group_row(x, group, idx):
        x3 = x.reshape(x.shape[0] // group, group, LANES)
        return jnp.broadcast_to(x3[:, idx:idx + 1, :], x3.shape).reshape(x.shape)

    chains = [(j, d) for j in range(HG_PAIR) for d in range(2)]
    tri = [(same_chunk & causal[d]).astype(BF16) for d in range(2)]
    diag_mask = [(r_sub == c_sub) & causal[d] for d in range(2)]

    def prepare_pair(gg, carry):
        sup = [gg * HG_PAIR + j for j in range(HG_PAIR)]
        rows = [pl.ds(pl.multiple_of(g * HG_SUPER, HG_SUPER), HG_SUPER) for g in sup]
        q = [qs[r, :] for r in rows]
        i16 = [iv[r, :] for r in rows]
        k = {(j, d): kk[d, rows[j], :] for j, d in chains}
        b = {(j, d): _sel_left(tri[d], lf[d, rows[j], :]) for j, d in chains}
        qh, keys, masks, expo = {}, {}, {}, {}
        for c in chains:
            j, d = c
            first = 0 if d == 0 else SUB - 1
            ref16 = group_row(b[c], SUB, first)
            expo[c] = ref16 - b[c]
            qh[c] = (q[j] * jnp.exp(b[c] - ref16)).astype(BF16)
            keys[c] = [(k[c] * jnp.exp(jnp.minimum(expo[c], HG_SAFE_EXP))).astype(BF16)]
            masks[c] = []
            for t_sub in (range(1, subs) if d == 0 else range(subs - 1)):
                ref_t = group_row(b[c], CHUNK, t_sub * SUB + first)
                src = (row_sub < t_sub) if d == 0 else (row_sub > t_sub)
                keys[c].append(
                    jnp.where(src, k[c] * jnp.exp(jnp.where(src, ref_t - b[c], 0.0)), 0.0).astype(BF16))
                masks[c].append(same_chunk & ((r_sub & (subs - 1)) == t_sub) & (
                    ((c_sub & (subs - 1)) < t_sub) if d == 0 else ((c_sub & (subs - 1)) > t_sub)))
        att_all = {c: _dot_nt(qh[c], jnp.concatenate(keys[c], axis=0)) for c in chains}
        att_off, att16 = {}, {}
        for c in chains:
            off = jnp.zeros((HG_SUPER, HG_SUPER), F32)
            for n, m in enumerate(masks[c]):
                off = off + jnp.where(m, att_all[c][:, (n + 1) * HG_SUPER:(n + 2) * HG_SUPER], 0.0)
            att_off[c] = off
            att16[c] = (off + jnp.where(diag_mask[c[1]], att_all[c][:, :HG_SUPER], 0.0)).astype(BF16)
        o_fast = {c: _dot(att16[c], i16[c[0]]) for c in chains}
        b_last, kt_bd = {}, {}
        for c in chains:
            j, d = c
            oi_scr[d, rows[j], :] = o_fast[c]
            b_last[c] = group_row(b[c], CHUNK, CHUNK - 1 if d == 0 else 0)
            qt_scr[d, rows[j], :] = (q[j] * jnp.exp(b[c])).astype(BF16)
            kt = k[c] * jnp.exp(b_last[c] - b[c])
            kt_bd[c] = jnp.concatenate(
                [jnp.where(row_blk == n, kt, 0.0) for n in range(per_sup)], axis=1).astype(BF16)
        nts = {c: _dot_tn(i16[c[0]], kt_bd[c]) for c in chains}
        for c in chains:
            j, d = c
            for n in range(per_sup):
                cc = sup[j] * per_sup + n
                nt_scr[d, cc] = nts[c][:, n * HEAD_DIM:(n + 1) * HEAD_DIM]
                fl_scr[d, cc] = jnp.exp(b_last[c][n * CHUNK:n * CHUNK + 1, :])

        for c in chains:
            j, d = c

            @pl.when(jnp.max(expo[c]) > HG_SAFE_EXP)
            def _(c=c, j=j, d=d):
                b3 = b[c].reshape(HG_SUPER // SUB, SUB, LANES)
                q3 = q[j].reshape(b3.shape)
                k3 = k[c].reshape(b3.shape)
                i3 = i16[j].astype(F32).reshape(b3.shape)
                acc = jnp.zeros(b3.shape, F32)
                for s in range(SUB):
                    ok = (pos16 >= s) if d == 0 else (pos16 <= s)
                    ex = jnp.exp(jnp.where(ok, b3 - b3[:, s:s + 1, :], 0.0))
                    col = jnp.sum(q3 * k3[:, s:s + 1, :] * ex, axis=-1, keepdims=True)
                    acc = acc + jnp.where(ok, col, 0.0) * i3[:, s:s + 1, :]
                oi_scr[d, rows[j], :] = (_dot(att_off[c].astype(BF16), i16[j])
                                         + acc.reshape(HG_SUPER, LANES))
        return carry

    lax.fori_loop(0, n_sup // HG_PAIR, prepare_pair, 0)

    st_scr[...] = jnp.zeros(st_scr.shape, F32)

    def scan_step(ii, carry):
        for u in range(HG_SCAN_UNROLL):
            i = ii * HG_SCAN_UNROLL + u
            for d in range(2):
                c = i if d == 0 else nc - 1 - i
                rows = pl.ds(pl.multiple_of(c * CHUNK, CHUNK), CHUNK)
                st = st_scr[d]
                oi_scr[d, rows, :] += _dot_nt(qt_scr[d, rows, :], st.astype(BF16))
                st_scr[d] = st * fl_scr[d, c] + nt_scr[d, c]
        return carry

    lax.fori_loop(0, nc // HG_SCAN_UNROLL, scan_step, 0)

    o = oi_scr[0] + oi_scr[1]
    y = o * lax.rsqrt(jnp.mean(o * o, axis=-1, keepdims=True) + EPS) * nw_ref[...]
    zz = hg_ref[...].astype(F32)
    o_ref[...] = (y * (zz * _sigmoid(zz))).astype(o_ref.dtype)


def _hgrn2(p_main, lb_logits, norm_w, layer, batch, s_len):
    nc = s_len // CHUNK
    assert s_len % HG_SUPER == 0
    depth = lb_logits.shape[0]
    tok = lambda tile: pl.BlockSpec((s_len, LANES), lambda b, h, tile=tile: (b, tile + h))
    return pl.pallas_call(
        functools.partial(_hg_kernel, layer),
        grid=(batch, N_HEADS),
        in_specs=[
            tok(T_HQ), tok(T_HFF), tok(T_HFB), tok(T_HI), tok(T_HGATE),
            pl.BlockSpec((depth, 2, LANES), lambda b, h: (0, 0, h)),
            pl.BlockSpec((1, LANES), lambda b, h: (0, 0)),
        ],
        out_specs=pl.BlockSpec((s_len, LANES), lambda b, h: (b, h)),
        out_shape=jax.ShapeDtypeStruct((batch * s_len, MIX_WIDTH), ACT_DTYPE),
        scratch_shapes=[
            pltpu.VMEM((s_len, LANES), F32),
            pltpu.VMEM((s_len, LANES), BF16),
            pltpu.VMEM((2, s_len, LANES), F32),
            pltpu.VMEM((2, s_len, LANES), F32),
            pltpu.VMEM((2, HEAD_DIM, HEAD_DIM), F32),
            pltpu.VMEM((2, s_len, LANES), F32),
            pltpu.VMEM((2, nc, HEAD_DIM, HEAD_DIM), F32),
            pltpu.VMEM((2, nc, 1, LANES), F32),
            pltpu.VMEM((2, s_len, LANES), BF16),
        ],
        compiler_params=pltpu.CompilerParams(
            dimension_semantics=("arbitrary", "arbitrary"), vmem_limit_bytes=VMEM_LIMIT),
        name="hgrn2",
    )(p_main, p_main, p_main, p_main, p_main, lb_logits, norm_w)


def _merge_kernel(x_ref, odn_ref, ohg_ref, gdn_ref, ghg_ref, wdn_ref, whg_ref, wo_ref, o_ref):
    a = _dot(odn_ref[...].astype(BF16), wdn_ref[...])
    b = _dot(ohg_ref[...].astype(BF16), whg_ref[...])
    merged = _sigmoid(gdn_ref[...].astype(F32)) * a + _sigmoid(ghg_ref[...].astype(F32)) * b
    o_ref[...] = x_ref[...] + _dot(merged.astype(BF16), wo_ref[...])


def _merge_out(x2d, o_dn, o_hg, p_main, w_dn, w_hg, w_out):
    t = x2d.shape[0]
    tm = min(1024, t)
    const = lambda shape: pl.BlockSpec(shape, lambda i: (0, 0))
    return pl.pallas_call(
        _merge_kernel,
        grid=(t // tm,),
        in_specs=[
            pl.BlockSpec((tm, D_MODEL), lambda i: (i, 0)),
            pl.BlockSpec((tm, MIX_WIDTH), lambda i: (i, 0)),
            pl.BlockSpec((tm, MIX_WIDTH), lambda i: (i, 0)),
            pl.BlockSpec((tm, D_MODEL), lambda i: (i, T_GATE_DN)),
            pl.BlockSpec((tm, D_MODEL), lambda i: (i, T_GATE_HG)),
            const((MIX_WIDTH, D_MODEL)), const((MIX_WIDTH, D_MODEL)), const((D_MODEL, D_MODEL)),
        ],
        out_specs=pl.BlockSpec((tm, D_MODEL), lambda i: (i, 0)),
        out_shape=jax.ShapeDtypeStruct((t, D_MODEL), F32),
        compiler_params=pltpu.CompilerParams(
            dimension_semantics=("arbitrary",), vmem_limit_bytes=VMEM_LIMIT),
        name="merge_out",
    )(x2d, o_dn, o_hg, p_main, p_main, w_dn, w_hg, w_out)


FF_COLS = D_FF // 2


def _ffn_kernel(final, xm_ref, xp_ref, xn_ref, nw_ref, wg_ref, wu_ref, cw_ref, cb_ref, wd_ref, fn_ref,
                o_ref, g_scr):
    i = pl.program_id(1)
    last = pl.num_programs(1) - 1
    ts = xm_ref.shape[1]
    x = xm_ref[0]
    h_ext = _rms(jnp.concatenate([xp_ref[0], x, xn_ref[0]], axis=0), nw_ref[...])
    h = h_ext[8:8 + ts].astype(BF16)
    h_ext = h_ext.astype(BF16)
    acc = x
    for c in range(D_FF // FF_COLS):
        cols = slice(c * FF_COLS, (c + 1) * FF_COLS)
        g_scr[...] = _dot(h_ext, wg_ref[:, cols])
        g_scr[7:8, :] = jnp.where(i > 0, g_scr[7:8, :], 0.0)
        g_scr[8 + ts:9 + ts, :] = jnp.where(i < last, g_scr[8 + ts:9 + ts, :], 0.0)
        gc = (cw_ref[0:1, cols] * g_scr[7:7 + ts, :] + cw_ref[1:2, cols] * g_scr[8:8 + ts, :]
              + cw_ref[2:3, cols] * g_scr[9:9 + ts, :] + cb_ref[:, cols])
        u = _dot(h, wu_ref[:, cols])
        act = (gc * _sigmoid(gc)) * u
        acc = acc + _dot(act.astype(BF16), wd_ref[cols, :])
    if final:
        acc = _rms(acc, fn_ref[...])
    o_ref[0] = acc


def _ffn(x3d, norm_w, w_gate, w_upp, conv_w, conv_b, w_down, final_w, final):
    batch, s_len, _ = x3d.shape
    ts = min(512, s_len)
    r8 = ts // 8
    n8 = s_len // 8
    const = lambda shape: pl.BlockSpec(shape, lambda b, i: (0, 0), pipeline_mode=pl.Buffered(1))
    return pl.pallas_call(
        functools.partial(_ffn_kernel, final),
        grid=(batch, s_len // ts),
        in_specs=[
            pl.BlockSpec((1, ts, D_MODEL), lambda b, i: (b, i, 0)),
            pl.BlockSpec((1, 8, D_MODEL), lambda b, i: (b, jnp.maximum(i * r8 - 1, 0), 0)),
            pl.BlockSpec((1, 8, D_MODEL), lambda b, i: (b, jnp.minimum((i + 1) * r8, n8 - 1), 0)),
            const((1, D_MODEL)),
            const((D_MODEL, D_FF)), const((D_MODEL, D_FF)),
            const((3, D_FF)), const((1, D_FF)),
            const((D_FF, D_MODEL)),
            const((1, D_MODEL)),
        ],
        out_specs=pl.BlockSpec((1, ts, D_MODEL), lambda b, i: (b, i, 0)),
        out_shape=jax.ShapeDtypeStruct(x3d.shape, F32),
        scratch_shapes=[pltpu.VMEM((ts + 16, FF_COLS), F32)],
        compiler_params=pltpu.CompilerParams(
            dimension_semantics=("arbitrary", "arbitrary"), vmem_limit_bytes=VMEM_LIMIT),
        name="ffn",
    )(x3d, x3d, x3d, norm_w, w_gate, w_upp, conv_w, conv_b, w_down, final_w)


def kernel(x, mix_norm, w_in, dn_conv, dn_a_log, dn_dt_bias, dn_norm, hg_lb_logits, hg_norm,
           w_branch_dn, w_branch_hg, w_out, ffn_norm, w_up, ffn_conv, ffn_conv_bias, w_down, final_norm):
    batch, s_len, _ = x.shape
    depth = mix_norm.shape[0]
    nc = s_len // CHUNK
    x2d = x.reshape(batch * s_len, D_MODEL)
    final_w = final_norm.reshape(1, D_MODEL)
    for l in range(depth):
        w = w_in[l]
        w_main = jnp.concatenate(
            [w[:, _C_GATE:], w[:, :_C_Z_END], w[:, _C_HQ:_C_GATE]], axis=1).astype(BF16)
        w_s = w[:, _C_Z_END:_C_HQ].astype(BF16)
        w_small = jnp.pad(w_s, ((0, 0), (0, LANES - N_SMALL)))
        p_main, small, small_t = _inproj(x2d, mix_norm[l].reshape(1, D_MODEL), w_main, w_small, w_s.T)
        small_t = small_t.reshape(N_SMALL, batch, s_len // DN_SUPER, DN_SUPER).transpose(1, 0, 2, 3)
        o_dn = _deltanet(p_main, small, small_t, dn_conv[l], dn_a_log[l], dn_dt_bias[l],
                         dn_norm[l].reshape(1, HEAD_DIM), batch, s_len)
        o_hg = _hgrn2(p_main, hg_lb_logits, hg_norm[l].reshape(1, HEAD_DIM), l, batch, s_len)
        x2d = _merge_out(x2d, o_dn, o_hg, p_main, w_branch_dn[l].astype(BF16),
                         w_branch_hg[l].astype(BF16), w_out[l].astype(BF16))
        wu = w_up[l].astype(BF16)
        x3d = _ffn(x2d.reshape(batch, s_len, D_MODEL), ffn_norm[l].reshape(1, D_MODEL),
                   wu[:, :D_FF], wu[:, D_FF:], ffn_conv[l], ffn_conv_bias[l].reshape(1, D_FF),
                   w_down[l].astype(BF16), final_w, l == depth - 1)
        x2d = x3d.reshape(batch * s_len, D_MODEL)
    return x2d.reshape(batch, s_len, D_MODEL)
```

```python
import functools

import jax
import jax.numpy as jnp
from jax import lax
from jax.experimental import pallas as pl
from jax.experimental.pallas import tpu as pltpu

F32 = jnp.float32
BF16 = jnp.bfloat16

LANES = 128
D_MODEL = 1024
N_HEADS = 4
HEAD_DIM = 128
MIX_WIDTH = N_HEADS * HEAD_DIM
DN_CONV = 5
D_FF = 2816
EPS = 1e-6
CHUNK = 64
SUB = 16
SUB_SHIFT = SUB.bit_length() - 1
HG_SUPER = 256
HG_PAIR = 2
HG_SCAN_UNROLL = 4
HG_SAFE_EXP = 60.0
CHUNK_SHIFT = CHUNK.bit_length() - 1
DN_SUPER = 256
DN_PAIR = 4
DN_SEQS = 2
ACT_DTYPE = BF16
VMEM_LIMIT = 56 * 1024 * 1024

N_MAIN = 2 * D_MODEL + 4 * MIX_WIDTH + 5 * MIX_WIDTH
T_GATE_DN, T_GATE_HG = 0, 1
T_Q, T_K, T_V, T_Z = 16, 20, 24, 28
T_HQ, T_HFF, T_HFB, T_HI, T_HGATE = 32, 36, 40, 44, 48
N_SMALL = 16

_C_Z_END = 4 * MIX_WIDTH
_C_HQ = _C_Z_END + N_SMALL
_C_GATE = _C_HQ + 5 * MIX_WIDTH


def _sigmoid(x):
    return 0.5 * jnp.tanh(0.5 * x) + 0.5


def _log1p_nonneg(x):
    return jnp.log(1.0 + x)


def _softplus(x):
    return jnp.maximum(x, 0.0) + _log1p_nonneg(jnp.exp(-jnp.abs(x)))


def _rms(x, w):
    return x * lax.rsqrt(jnp.mean(x * x, axis=-1, keepdims=True) + EPS) * w


def _dot(a, b):
    return jnp.dot(a, b, preferred_element_type=F32)


def _dot_nt(a, b):
    return lax.dot_general(a, b, (((1,), (1,)), ((), ())), preferred_element_type=F32)


def _dot_tn(a, b):
    return lax.dot_general(a, b, (((0,), (0,)), ((), ())), preferred_element_type=F32)


def _split3(x):
    hi = x.astype(BF16)
    r = x - hi.astype(F32)
    mid = r.astype(BF16)
    lo = (r - mid.astype(F32)).astype(BF16)
    return hi, mid, lo


def _sel_left(m16, x):
    w = x.shape[1]
    y = _dot(m16, jnp.concatenate(_split3(x), axis=1))
    return y[:, :w] + y[:, w:2 * w] + y[:, 2 * w:]


def _sel_right(x, m16):
    r = x.shape[0]
    y = _dot(jnp.concatenate(_split3(x), axis=0), m16)
    return y[:r] + y[r:2 * r] + y[2 * r:]


def _tri(n, lower, strict=False):
    r = lax.broadcasted_iota(jnp.int32, (n, n), 0)
    c = lax.broadcasted_iota(jnp.int32, (n, n), 1)
    if lower:
        return (r > c) if strict else (r >= c)
    return (r < c) if strict else (r <= c)


def _inproj_kernel(x_ref, nw_ref, w_ref, ws_ref, wst_ref, p_ref, s_ref, st_ref, h_ref):
    @pl.when(pl.program_id(1) == 0)
    def _():
        hb = _rms(x_ref[...], nw_ref[...]).astype(BF16)
        h_ref[...] = hb
        s_ref[...] = _dot(hb, ws_ref[...])
        st_ref[...] = _dot_nt(wst_ref[...], hb)

    p_ref[...] = _dot(h_ref[...], w_ref[...]).astype(p_ref.dtype)


def _inproj(x2d, norm_w, w_main, w_small, w_small_t):
    t = x2d.shape[0]
    tm = min(1024, t)
    tn = N_MAIN // 2
    return pl.pallas_call(
        _inproj_kernel,
        grid=(t // tm, N_MAIN // tn),
        in_specs=[
            pl.BlockSpec((tm, D_MODEL), lambda i, j: (i, 0)),
            pl.BlockSpec((1, D_MODEL), lambda i, j: (0, 0)),
            pl.BlockSpec((D_MODEL, tn), lambda i, j: (0, j)),
            pl.BlockSpec((D_MODEL, LANES), lambda i, j: (0, 0)),
            pl.BlockSpec((N_SMALL, D_MODEL), lambda i, j: (0, 0)),
        ],
        out_specs=[
            pl.BlockSpec((tm, tn), lambda i, j: (i, j)),
            pl.BlockSpec((tm, LANES), lambda i, j: (i, 0)),
            pl.BlockSpec((N_SMALL, tm), lambda i, j: (0, i)),
        ],
        out_shape=[
            jax.ShapeDtypeStruct((t, N_MAIN), ACT_DTYPE),
            jax.ShapeDtypeStruct((t, LANES), F32),
            jax.ShapeDtypeStruct((N_SMALL, t), F32),
        ],
        scratch_shapes=[pltpu.VMEM((tm, D_MODEL), BF16)],
        compiler_params=pltpu.CompilerParams(
            dimension_semantics=("arbitrary", "arbitrary"), vmem_limit_bytes=VMEM_LIMIT),
        name="inproj",
    )(x2d, norm_w, w_main, w_small, w_small_t)


def _neumann_inverses(mats, block):
    n = mats[0].shape[0]
    nb = n // block
    shift = block.bit_length() - 1
    steps = shift
    r_p = lax.broadcasted_iota(jnp.int32, (block, n), 0)
    c_p = lax.broadcasted_iota(jnp.int32, (block, n), 1)
    c_blk = lax.shift_right_logical(c_p, shift)
    r_n = lax.broadcasted_iota(jnp.int32, (n, n), 0)
    c_n = lax.broadcasted_iota(jnp.int32, (n, n), 1)
    same = lax.shift_right_logical(r_n, shift) == lax.shift_right_logical(c_n, shift)

    def pack(m):
        out = m[0:block]
        for j in range(1, nb):
            out = jnp.where(c_blk == j, m[j * block:(j + 1) * block], out)
        return out

    def unpack(p):
        return jnp.where(same, jnp.concatenate([p] * nb, axis=0), 0.0)

    eye_p = (r_p == (c_p & (block - 1))).astype(F32)
    xs = [pack(a) for a in mats]
    ss = [eye_p - x for x in xs]
    xs = [_dot(x.astype(BF16), a.astype(BF16)) for x, a in zip(xs, mats)]
    for k in range(1, steps):
        x_bd = [unpack(x).astype(BF16) for x in xs]
        if k < steps - 1:
            lhs = [jnp.concatenate([s, x], axis=0).astype(BF16) for s, x in zip(ss, xs)]
            both = [_dot(l, xb) for l, xb in zip(lhs, x_bd)]
            ss = [s + b[:block] for s, b in zip(ss, both)]
            xs = [b[block:] for b in both]
        else:
            s16 = [s.astype(BF16) for s in ss]
            ss = [s + _dot(l, xb) for s, l, xb in zip(ss, s16, x_bd)]
    return [unpack(s) for s in ss]


def _dn_kernel(seqs, alog_ref, dtb_ref, q_ref, k_ref, v_ref, z_ref, small_ref, smallt_ref,
               cq_ref, ck_ref, cv_ref, nw_ref, o_ref,
               qpad, kpad, vpad, stg, ball, gcall, grow, st_scr, of_scr, mt_scr, nt_scr, qh_scr,
               gl_scr):
    h = pl.program_id(1)
    s_len = q_ref.shape[0] // seqs
    nc = s_len // CHUNK
    lane = lax.broadcasted_iota(jnp.int32, (1, LANES), 1)

    for pad, x_ref in ((qpad, q_ref), (kpad, k_ref), (vpad, v_ref)):
        for s in range(seqs):
            pad[s, 0:8, :] = jnp.zeros((8, LANES), F32)
            pad[s, 8 + s_len:16 + s_len, :] = jnp.zeros((8, LANES), F32)
            pad[s, 8:8 + s_len, :] = x_ref[s * s_len:(s + 1) * s_len, :].astype(F32)

    def conv_silu(pad, cw_ref, seq, base):
        win = pad[seq, pl.ds(base, DN_SUPER + 16), :]
        y = cw_ref[0:1, :] * win[6:6 + DN_SUPER, :]
        for j in range(1, DN_CONV):
            y = y + cw_ref[j:j + 1, :] * win[6 + j:6 + j + DN_SUPER, :]
        return y * _sigmoid(y)

    def l2n(y):
        return y * lax.rsqrt(jnp.sum(y * y, axis=-1, keepdims=True) + EPS)

    n_sup = s_len // DN_SUPER
    n_tot = seqs * n_sup
    per_sup = DN_SUPER // CHUNK
    r_i = lax.broadcasted_iota(jnp.int32, (DN_SUPER, DN_SUPER), 0)
    c_i = lax.broadcasted_iota(jnp.int32, (DN_SUPER, DN_SUPER), 1)
    same = lax.shift_right_logical(r_i, CHUNK_SHIFT) == lax.shift_right_logical(c_i, CHUNK_SHIFT)
    incl = (same & (r_i >= c_i), same & (r_i <= c_i))
    strict = (same & (r_i > c_i), same & (r_i < c_i))

    @pl.when(h == 0)
    def _():
        alv = jnp.zeros((1, LANES), F32)
        dtv = jnp.zeros((1, LANES), F32)
        for d in range(2):
            for hh in range(N_HEADS):
                alv = jnp.where(lane == 8 + d * N_HEADS + hh, alog_ref[d, hh], alv)
                dtv = jnp.where(lane == 8 + d * N_HEADS + hh, dtb_ref[d, hh], dtv)
        sm = small_ref[...]
        ball[...] = _sigmoid(sm)
        gcall[...] = -jnp.exp(alv) * _softplus(sm + dtv)
        tri_both = jnp.concatenate([incl[0].astype(BF16), incl[1].astype(BF16)], axis=0)
        fwd_lane = lane < 8 + N_HEADS

        def cumulate(g, carry):
            rows = pl.ds(pl.multiple_of(g * DN_SUPER, DN_SUPER), DN_SUPER)
            both = _sel_left(tri_both, gcall[rows, :])
            gcall[rows, :] = jnp.where(fwd_lane, both[:DN_SUPER], both[DN_SUPER:])
            return carry

        lax.fori_loop(0, n_tot, cumulate, 0)

    def pick(x, idx):
        col = jnp.sum(jnp.where(lane == idx, x, 0.0), axis=-1, keepdims=True)
        return jnp.broadcast_to(col, x.shape)

    for d in range(2):
        al = jnp.full((1, DN_SUPER), alog_ref[d, h], F32)
        dt = jnp.full((1, DN_SUPER), dtb_ref[d, h], F32)
        for s in range(seqs):
            g_r = -jnp.exp(al) * _softplus(smallt_ref[s, 8 + d * N_HEADS + h] + dt)
            grow[d, s * n_sup:(s + 1) * n_sup, :] = _sel_right(g_r, incl[1 - d].astype(BF16))

    row_blk = lax.shift_right_logical(lax.broadcasted_iota(jnp.int32, (DN_SUPER, 1), 0), CHUNK_SHIFT)

    def stage(g, p, j):
        seq = lax.div(g, n_sup)
        base = pl.multiple_of(lax.rem(g, n_sup) * DN_SUPER, DN_SUPER)
        rows = pl.ds(pl.multiple_of(g * DN_SUPER, DN_SUPER), DN_SUPER)
        stg[p, j, 0] = l2n(conv_silu(qpad, cq_ref, seq, base)) * (HEAD_DIM ** -0.5)
        stg[p, j, 1] = l2n(conv_silu(kpad, ck_ref, seq, base))
        stg[p, j, 2] = conv_silu(vpad, cv_ref, seq, base)
        for d in range(2):
            stg[p, j, 3 + 2 * d] = pick(gcall[rows, :], 8 + d * N_HEADS + h)
            stg[p, j, 4 + 2 * d] = pick(ball[rows, :], d * N_HEADS + h)

    n_pairs = n_tot // DN_PAIR
    chains = [(j, d) for j in range(DN_PAIR) for d in range(2)]

    def prepare_pair(gg, carry):
        p = gg & 1
        nxt = jnp.minimum(gg + 1, n_pairs - 1)
        for j in range(DN_PAIR):
            stage(nxt * DN_PAIR + j, 1 - p, j)
        q = [stg[p, j, 0] for j in range(DN_PAIR)]
        k = [stg[p, j, 1] for j in range(DN_PAIR)]
        v = [stg[p, j, 2] for j in range(DN_PAIR)]
        k16 = [x.astype(BF16) for x in k]
        kk = [_dot_nt(x, x) for x in k16]
        qk_raw = [_dot_nt(x.astype(BF16), y) for x, y in zip(q, k16)]
        gc_c, be_c, decay, a_mats = {}, {}, {}, []
        for j, d in chains:
            gc_c[j, d] = stg[p, j, 3 + 2 * d]
            be_c[j, d] = stg[p, j, 4 + 2 * d]
            gc_r = grow[d, pl.ds(gg * DN_PAIR + j, 1), :]
            diff = jnp.concatenate([gc_c[j, d]] * 2, axis=1) - gc_r
            decay[j, d] = jnp.where(incl[d], jnp.exp(jnp.where(incl[d], diff, 0.0)), 0.0)
            a_mats.append(jnp.where(strict[d], kk[j] * jnp.concatenate([be_c[j, d]] * 2, axis=1) * decay[j, d], 0.0))
        t_inv = _neumann_inverses(a_mats, CHUNK)
        e_gc = {c: jnp.exp(gc_c[c]) for c in chains}
        rhs = [jnp.concatenate([v[j] * be_c[j, d], k[j] * (be_c[j, d] * e_gc[j, d])], axis=1).astype(BF16)
               for j, d in chains]
        uw16 = [_dot(t.astype(BF16), r).astype(BF16) for t, r in zip(t_inv, rhs)]
        qk16 = [(qk_raw[j] * decay[j, d]).astype(BF16) for j, d in chains]
        qu_qw = [_dot(x, y) for x, y in zip(qk16, uw16)]
        gc_last, kd_bd = [], []
        for j, d in chains:
            last = CHUNK - 1 if d == 0 else 0
            gl = [gc_c[j, d][i * CHUNK + last:i * CHUNK + last + 1, :] for i in range(per_sup)]
            gcl = jnp.concatenate([jnp.broadcast_to(x, (CHUNK, LANES)) for x in gl], axis=0)
            kd = k[j] * jnp.exp(gcl - gc_c[j, d])
            gc_last.append(gl)
            kd_bd.append(jnp.concatenate(
                [jnp.where(row_blk == i, kd, 0.0) for i in range(per_sup)], axis=1).astype(BF16))
        mn = [_dot_tn(x, y) for x, y in zip(uw16, kd_bd)]
        for n, (j, d) in enumerate(chains):
            g = gg * DN_PAIR + j
            rows = pl.ds(pl.multiple_of(g * DN_SUPER, DN_SUPER), DN_SUPER)
            for i in range(per_sup):
                c = g * per_sup + i
                cols = slice(i * HEAD_DIM, (i + 1) * HEAD_DIM)
                nt_scr[d, c] = mn[n][:HEAD_DIM, cols]
                mt_scr[d, c] = (-mn[n][HEAD_DIM:, cols]).astype(BF16)
                gl_scr[d, c] = jnp.exp(gc_last[n][i])
            qh_scr[d, rows, :] = (q[j] * e_gc[j, d] - qu_qw[n][:, HEAD_DIM:]).astype(BF16)
            if d == 1:
                of_scr[rows, :] = qu_qw[n - 1][:, :HEAD_DIM] + qu_qw[n][:, :HEAD_DIM]
        return carry

    for j in range(DN_PAIR):
        stage(j, 0, j)
    lax.fori_loop(0, n_pairs, prepare_pair, 0)

    st_scr[...] = jnp.zeros(st_scr.shape, F32)

    def scan_step(i, carry):
        for s in range(seqs):
            for d in range(2):
                c = s * nc + (i if d == 0 else nc - 1 - i)
                rows = pl.ds(pl.multiple_of(c * CHUNK, CHUNK), CHUNK)
                st = st_scr[s, d]
                st16 = st.astype(BF16)
                of_scr[rows, :] += _dot_nt(qh_scr[d, rows, :], st16)
                st_scr[s, d] = st * gl_scr[d, c] + _dot(st16, mt_scr[d, c]) + nt_scr[d, c]
        return carry

    lax.fori_loop(0, nc, scan_step, 0)

    o = of_scr[...]
    y = o * lax.rsqrt(jnp.mean(o * o, axis=-1, keepdims=True) + EPS) * nw_ref[...]
    zz = z_ref[...].astype(F32)
    o_ref[...] = (y * (zz * _sigmoid(zz))).astype(o_ref.dtype)


def _deltanet(p_main, small, small_t, conv_w, a_log, dt_bias, norm_w, batch, s_len):
    seqs = DN_SEQS if batch % DN_SEQS == 0 else 1
    n_rows = seqs * s_len
    nc = n_rows // CHUNK
    n_sup = s_len // DN_SUPER
    assert s_len % DN_SUPER == 0 and (seqs * n_sup) % DN_PAIR == 0
    tok = lambda tile: pl.BlockSpec((n_rows, LANES), lambda b, h, tile=tile: (b, tile + h))
    cw = lambda tile: pl.BlockSpec((DN_CONV, LANES), lambda b, h, tile=tile: (0, tile + h))
    smem = pl.BlockSpec(memory_space=pltpu.SMEM)
    return pl.pallas_call(
        functools.partial(_dn_kernel, seqs),
        grid=(batch // seqs, N_HEADS),
        in_specs=[
            smem, smem,
            tok(T_Q), tok(T_K), tok(T_V), tok(T_Z),
            pl.BlockSpec((n_rows, LANES), lambda b, h: (b, 0)),
            pl.BlockSpec((seqs, N_SMALL, n_sup, DN_SUPER), lambda b, h: (b, 0, 0, 0)),
            cw(0), cw(N_HEADS), cw(2 * N_HEADS),
            pl.BlockSpec((1, LANES), lambda b, h: (0, 0)),
        ],
        out_specs=pl.BlockSpec((n_rows, LANES), lambda b, h: (b, h)),
        out_shape=jax.ShapeDtypeStruct((batch * s_len, MIX_WIDTH), ACT_DTYPE),
        scratch_shapes=[
            pltpu.VMEM((seqs, s_len + 16, LANES), F32),
            pltpu.VMEM((seqs, s_len + 16, LANES), F32),
            pltpu.VMEM((seqs, s_len + 16, LANES), F32),
            pltpu.VMEM((2, DN_PAIR, 7, DN_SUPER, LANES), F32),
            pltpu.VMEM((n_rows, LANES), F32),
            pltpu.VMEM((n_rows, LANES), F32),
            pltpu.VMEM((2, seqs * n_sup, DN_SUPER), F32),
            pltpu.VMEM((seqs, 2, HEAD_DIM, HEAD_DIM), F32),
            pltpu.VMEM((n_rows, LANES), F32),
            pltpu.VMEM((2, nc, HEAD_DIM, HEAD_DIM), BF16),
            pltpu.VMEM((2, nc, HEAD_DIM, HEAD_DIM), F32),
            pltpu.VMEM((2, n_rows, LANES), BF16),
            pltpu.VMEM((2, nc, 1, LANES), F32),
        ],
        compiler_params=pltpu.CompilerParams(
            dimension_semantics=("arbitrary", "arbitrary"), vmem_limit_bytes=VMEM_LIMIT),
        name="deltanet",
    )(a_log, dt_bias, p_main, p_main, p_main, p_main, small, small_t,
      conv_w, conv_w, conv_w, norm_w)


def _hg_kernel(layer, hq_ref, hff_ref, hfb_ref, hi_ref, hg_ref, lbl_ref, nw_ref, o_ref,
               qs, iv, lf, kk, st_scr, oi_scr, nt_scr, fl_scr, qt_scr):
    s_len = hq_ref.shape[0]
    nc = s_len // CHUNK

    lg = lbl_ref[...].astype(F32)
    e = jnp.exp(lg - jnp.max(lg, axis=0, keepdims=True))
    sm = e / jnp.sum(e, axis=0, keepdims=True)
    lb = jnp.clip(jnp.sum(sm[:layer + 1], axis=0) - sm[0], 0.0, 1.0)

    x = hq_ref[...].astype(F32)
    qs[...] = (x * _sigmoid(x)) * (HEAD_DIM ** -0.5)
    iv[...] = hi_ref[...].astype(BF16)
    for d, ref in enumerate((hff_ref, hfb_ref)):
        fr = ref[...].astype(F32)
        lbd = lb[d:d + 1, :]
        log_sig = jnp.minimum(fr, 0.0) - _log1p_nonneg(jnp.exp(-jnp.abs(fr)))
        lf[d] = log_sig + _log1p_nonneg(lbd * jnp.exp(-fr))
        kk[d] = (1.0 - lbd) * _sigmoid(-fr)

    n_sup = s_len // HG_SUPER
    per_sup = HG_SUPER // CHUNK
    subs = CHUNK // SUB
    r_i = lax.broadcasted_iota(jnp.int32, (HG_SUPER, HG_SUPER), 0)
    c_i = lax.broadcasted_iota(jnp.int32, (HG_SUPER, HG_SUPER), 1)
    same_chunk = lax.shift_right_logical(r_i, CHUNK_SHIFT) == lax.shift_right_logical(c_i, CHUNK_SHIFT)
    r_sub = lax.shift_right_logical(r_i, SUB_SHIFT)
    c_sub = lax.shift_right_logical(c_i, SUB_SHIFT)
    causal = (r_i >= c_i, r_i <= c_i)
    row_i = lax.broadcasted_iota(jnp.int32, (HG_SUPER, 1), 0)
    row_sub = lax.shift_right_logical(row_i, SUB_SHIFT) & (subs - 1)
    row_blk = lax.shift_right_logical(row_i, CHUNK_SHIFT)
    pos16 = lax.broadcasted_iota(jnp.int32, (1, SUB, 1), 1)

    def group_row(x, group, idx):
        x3 = x.reshape(x.shape[0] // group, group, LANES)
        return jnp.broadcast_to(x3[:, idx:idx + 1, :], x3.shape).reshape(x.shape)

    chains = [(j, d) for j in range(HG_PAIR) for d in range(2)]
    tri = [(same_chunk & causal[d]).astype(BF16) for d in range(2)]
    diag_mask = [(r_sub == c_sub) & causal[d] for d in range(2)]

    def prepare_pair(gg, carry):
        sup = [gg * HG_PAIR + j for j in range(HG_PAIR)]
        rows = [pl.ds(pl.multiple_of(g * HG_SUPER, HG_SUPER), HG_SUPER) for g in sup]
        q = [qs[r, :] for r in rows]
        i16 = [iv[r, :] for r in rows]
        k = {(j, d): kk[d, rows[j], :] for j, d in chains}
        b = {(j, d): _sel_left(tri[d], lf[d, rows[j], :]) for j, d in chains}
        qh, keys, masks, expo = {}, {}, {}, {}
        for c in chains:
            j, d = c
            first = 0 if d == 0 else SUB - 1
            ref16 = group_row(b[c], SUB, first)
            expo[c] = ref16 - b[c]
            qh[c] = (q[j] * jnp.exp(b[c] - ref16)).astype(BF16)
            keys[c] = [(k[c] * jnp.exp(jnp.minimum(expo[c], HG_SAFE_EXP))).astype(BF16)]
            masks[c] = []
            for t_sub in (range(1, subs) if d == 0 else range(subs - 1)):
                ref_t = group_row(b[c], CHUNK, t_sub * SUB + first)
                src = (row_sub < t_sub) if d == 0 else (row_sub > t_sub)
                keys[c].append(
                    jnp.where(src, k[c] * jnp.exp(jnp.where(src, ref_t - b[c], 0.0)), 0.0).astype(BF16))
                masks[c].append(same_chunk & ((r_sub & (subs - 1)) == t_sub) & (
                    ((c_sub & (subs - 1)) < t_sub) if d == 0 else ((c_sub & (subs - 1)) > t_sub)))
        att_all = {c: _dot_nt(qh[c], jnp.concatenate(keys[c], axis=0)) for c in chains}
        att_off, att16 = {}, {}
        for c in chains:
            off = jnp.zeros((HG_SUPER, HG_SUPER), F32)
            for n, m in enumerate(masks[c]):
                off = off + jnp.where(m, att_all[c][:, (n + 1) * HG_SUPER:(n + 2) * HG_SUPER], 0.0)
            att_off[c] = off
            att16[c] = (off + jnp.where(diag_mask[c[1]], att_all[c][:, :HG_SUPER], 0.0)).astype(BF16)
        o_fast = {c: _dot(att16[c], i16[c[0]]) for c in chains}
        b_last, kt_bd = {}, {}
        for c in chains:
            j, d = c
            oi_scr[d, rows[j], :] = o_fast[c]
            b_last[c] = group_row(b[c], CHUNK, CHUNK - 1 if d == 0 else 0)
            qt_scr[d, rows[j], :] = (q[j] * jnp.exp(b[c])).astype(BF16)
            kt = k[c] * jnp.exp(b_last[c] - b[c])
            kt_bd[c] = jnp.concatenate(
                [jnp.where(row_blk == n, kt, 0.0) for n in range(per_sup)], axis=1).astype(BF16)
        nts = {c: _dot_tn(i16[c[0]], kt_bd[c]) for c in chains}
        for c in chains:
            j, d = c
            for n in range(per_sup):
                cc = sup[j] * per_sup + n
                nt_scr[d, cc] = nts[c][:, n * HEAD_DIM:(n + 1) * HEAD_DIM]
                fl_scr[d, cc] = jnp.exp(b_last[c][n * CHUNK:n * CHUNK + 1, :])

        for c in chains:
            j, d = c

            @pl.when(jnp.max(expo[c]) > HG_SAFE_EXP)
            def _(c=c, j=j, d=d):
                b3 = b[c].reshape(HG_SUPER // SUB, SUB, LANES)
                q3 = q[j].reshape(b3.shape)
                k3 = k[c].reshape(b3.shape)
                i3 = i16[j].astype(F32).reshape(b3.shape)
                acc = jnp.zeros(b3.shape, F32)
                for s in range(SUB):
                    ok = (pos16 >= s) if d == 0 else (pos16 <= s)
                    ex = jnp.exp(jnp.where(ok, b3 - b3[:, s:s + 1, :], 0.0))
                    col = jnp.sum(q3 * k3[:, s:s + 1, :] * ex, axis=-1, keepdims=True)
                    acc = acc + jnp.where(ok, col, 0.0) * i3[:, s:s + 1, :]
                oi_scr[d, rows[j], :] = (_dot(att_off[c].astype(BF16), i16[j])
                                         + acc.reshape(HG_SUPER, LANES))
        return carry

    lax.fori_loop(0, n_sup // HG_PAIR, prepare_pair, 0)

    st_scr[...] = jnp.zeros(st_scr.shape, F32)

    def scan_step(ii, carry):
        for u in range(HG_SCAN_UNROLL):
            i = ii * HG_SCAN_UNROLL + u
            for d in range(2):
                c = i if d == 0 else nc - 1 - i
                rows = pl.ds(pl.multiple_of(c * CHUNK, CHUNK), CHUNK)
                st = st_scr[d]
                oi_scr[d, rows, :] += _dot_nt(qt_scr[d, rows, :], st.astype(BF16))
                st_scr[d] = st * fl_scr[d, c] + nt_scr[d, c]
        return carry

    lax.fori_loop(0, nc // HG_SCAN_UNROLL, scan_step, 0)

    o = oi_scr[0] + oi_scr[1]
    y = o * lax.rsqrt(jnp.mean(o * o, axis=-1, keepdims=True) + EPS) * nw_ref[...]
    zz = hg_ref[...].astype(F32)
    o_ref[...] = (y * (zz * _sigmoid(zz))).astype(o_ref.dtype)


def _hgrn2(p_main, lb_logits, norm_w, layer, batch, s_len):
    nc = s_len // CHUNK
    assert s_len % HG_SUPER == 0
    depth = lb_logits.shape[0]
    tok = lambda tile: pl.BlockSpec((s_len, LANES), lambda b, h, tile=tile: (b, tile + h))
    return pl.pallas_call(
        functools.partial(_hg_kernel, layer),
        grid=(batch, N_HEADS),
        in_specs=[
            tok(T_HQ), tok(T_HFF), tok(T_HFB), tok(T_HI), tok(T_HGATE),
            pl.BlockSpec((depth, 2, LANES), lambda b, h: (0, 0, h)),
            pl.BlockSpec((1, LANES), lambda b, h: (0, 0)),
        ],
        out_specs=pl.BlockSpec((s_len, LANES), lambda b, h: (b, h)),
        out_shape=jax.ShapeDtypeStruct((batch * s_len, MIX_WIDTH), ACT_DTYPE),
        scratch_shapes=[
            pltpu.VMEM((s_len, LANES), F32),
            pltpu.VMEM((s_len, LANES), BF16),
            pltpu.VMEM((2, s_len, LANES), F32),
            pltpu.VMEM((2, s_len, LANES), F32),
            pltpu.VMEM((2, HEAD_DIM, HEAD_DIM), F32),
            pltpu.VMEM((2, s_len, LANES), F32),
            pltpu.VMEM((2, nc, HEAD_DIM, HEAD_DIM), F32),
            pltpu.VMEM((2, nc, 1, LANES), F32),
            pltpu.VMEM((2, s_len, LANES), BF16),
        ],
        compiler_params=pltpu.CompilerParams(
            dimension_semantics=("arbitrary", "arbitrary"), vmem_limit_bytes=VMEM_LIMIT),
        name="hgrn2",
    )(p_main, p_main, p_main, p_main, p_main, lb_logits, norm_w)


def _merge_kernel(x_ref, odn_ref, ohg_ref, gdn_ref, ghg_ref, wdn_ref, whg_ref, wo_ref, o_ref):
    a = _dot(odn_ref[...].astype(BF16), wdn_ref[...])
    b = _dot(ohg_ref[...].astype(BF16), whg_ref[...])
    merged = _sigmoid(gdn_ref[...].astype(F32)) * a + _sigmoid(ghg_ref[...].astype(F32)) * b
    o_ref[...] = x_ref[...] + _dot(merged.astype(BF16), wo_ref[...])


def _merge_out(x2d, o_dn, o_hg, p_main, w_dn, w_hg, w_out):
    t = x2d.shape[0]
    tm = min(1024, t)
    const = lambda shape: pl.BlockSpec(shape, lambda i: (0, 0))
    return pl.pallas_call(
        _merge_kernel,
        grid=(t // tm,),
        in_specs=[
            pl.BlockSpec((tm, D_MODEL), lambda i: (i, 0)),
            pl.BlockSpec((tm, MIX_WIDTH), lambda i: (i, 0)),
            pl.BlockSpec((tm, MIX_WIDTH), lambda i: (i, 0)),
            pl.BlockSpec((tm, D_MODEL), lambda i: (i, T_GATE_DN)),
            pl.BlockSpec((tm, D_MODEL), lambda i: (i, T_GATE_HG)),
            const((MIX_WIDTH, D_MODEL)), const((MIX_WIDTH, D_MODEL)), const((D_MODEL, D_MODEL)),
        ],
        out_specs=pl.BlockSpec((tm, D_MODEL), lambda i: (i, 0)),
        out_shape=jax.ShapeDtypeStruct((t, D_MODEL), F32),
        compiler_params=pltpu.CompilerParams(
            dimension_semantics=("arbitrary",), vmem_limit_bytes=VMEM_LIMIT),
        name="merge_out",
    )(x2d, o_dn, o_hg, p_main, p_main, w_dn, w_hg, w_out)


FF_COLS = D_FF // 2


def _ffn_kernel(final, xm_ref, xp_ref, xn_ref, nw_ref, wg_ref, wu_ref, cw_ref, cb_ref, wd_ref, fn_ref,
                o_ref, g_scr):
    i = pl.program_id(1)
    last = pl.num_programs(1) - 1
    ts = xm_ref.shape[1]
    x = xm_ref[0]
    h_ext = _rms(jnp.concatenate([xp_ref[0], x, xn_ref[0]], axis=0), nw_ref[...])
    h = h_ext[8:8 + ts].astype(BF16)
    h_ext = h_ext.astype(BF16)
    acc = x
    for c in range(D_FF // FF_COLS):
        cols = slice(c * FF_COLS, (c + 1) * FF_COLS)
        g_scr[...] = _dot(h_ext, wg_ref[:, cols])
        g_scr[7:8, :] = jnp.where(i > 0, g_scr[7:8, :], 0.0)
        g_scr[8 + ts:9 + ts, :] = jnp.where(i < last, g_scr[8 + ts:9 + ts, :], 0.0)
        gc = (cw_ref[0:1, cols] * g_scr[7:7 + ts, :] + cw_ref[1:2, cols] * g_scr[8:8 + ts, :]
              + cw_ref[2:3, cols] * g_scr[9:9 + ts, :] + cb_ref[:, cols])
        u = _dot(h, wu_ref[:, cols])
        act = (gc * _sigmoid(gc)) * u
        acc = acc + _dot(act.astype(BF16), wd_ref[cols, :])
    if final:
        acc = _rms(acc, fn_ref[...])
    o_ref[0] = acc


def _ffn(x3d, norm_w, w_gate, w_upp, conv_w, conv_b, w_down, final_w, final):
    batch, s_len, _ = x3d.shape
    ts = min(512, s_len)
    r8 = ts // 8
    n8 = s_len // 8
    const = lambda shape: pl.BlockSpec(shape, lambda b, i: (0, 0), pipeline_mode=pl.Buffered(1))
    return pl.pallas_call(
        functools.partial(_ffn_kernel, final),
        grid=(batch, s_len // ts),
        in_specs=[
            pl.BlockSpec((1, ts, D_MODEL), lambda b, i: (b, i, 0)),
            pl.BlockSpec((1, 8, D_MODEL), lambda b, i: (b, jnp.maximum(i * r8 - 1, 0), 0)),
            pl.BlockSpec((1, 8, D_MODEL), lambda b, i: (b, jnp.minimum((i + 1) * r8, n8 - 1), 0)),
            const((1, D_MODEL)),
            const((D_MODEL, D_FF)), const((D_MODEL, D_FF)),
            const((3, D_FF)), const((1, D_FF)),
            const((D_FF, D_MODEL)),
            const((1, D_MODEL)),
        ],
        out_specs=pl.BlockSpec((1, ts, D_MODEL), lambda b, i: (b, i, 0)),
        out_shape=jax.ShapeDtypeStruct(x3d.shape, F32),
        scratch_shapes=[pltpu.VMEM((ts + 16, FF_COLS), F32)],
        compiler_params=pltpu.CompilerParams(
            dimension_semantics=("arbitrary", "arbitrary"), vmem_limit_bytes=VMEM_LIMIT),
        name="ffn",
    )(x3d, x3d, x3d, norm_w, w_gate, w_upp, conv_w, conv_b, w_down, final_w)


def kernel(x, mix_norm, w_in, dn_conv, dn_a_log, dn_dt_bias, dn_norm, hg_lb_logits, hg_norm,
           w_branch_dn, w_branch_hg, w_out, ffn_norm, w_up, ffn_conv, ffn_conv_bias, w_down, final_norm):
    batch, s_len, _ = x.shape
    depth = mix_norm.shape[0]
    nc = s_len // CHUNK
    x2d = x.reshape(batch * s_len, D_MODEL)
    final_w = final_norm.reshape(1, D_MODEL)
    for l in range(depth):
        w = w_in[l]
        w_main = jnp.concatenate(
            [w[:, _C_GATE:], w[:, :_C_Z_END], w[:, _C_HQ:_C_GATE]], axis=1).astype(BF16)
        w_s = w[:, _C_Z_END:_C_HQ].astype(BF16)
        w_small = jnp.pad(w_s, ((0, 0), (0, LANES - N_SMALL)))
        p_main, small, small_t = _inproj(x2d, mix_norm[l].reshape(1, D_MODEL), w_main, w_small, w_s.T)
        small_t = small_t.reshape(N_SMALL, batch, s_len // DN_SUPER, DN_SUPER).transpose(1, 0, 2, 3)
        o_dn = _deltanet(p_main, small, small_t, dn_conv[l], dn_a_log[l], dn_dt_bias[l],
                         dn_norm[l].reshape(1, HEAD_DIM), batch, s_len)
        o_hg = _hgrn2(p_main, hg_lb_logits, hg_norm[l].reshape(1, HEAD_DIM), l, batch, s_len)
        x2d = _merge_out(x2d, o_dn, o_hg, p_main, w_branch_dn[l].astype(BF16),
                         w_branch_hg[l].astype(BF16), w_out[l].astype(BF16))
        wu = w_up[l].astype(BF16)
        x3d = _ffn(x2d.reshape(batch, s_len, D_MODEL), ffn_norm[l].reshape(1, D_MODEL),
                   wu[:, :D_FF], wu[:, D_FF:], ffn_conv[l], ffn_conv_bias[l].reshape(1, D_FF),
                   w_down[l].astype(BF16), final_w, l == depth - 1)
        x2d = x3d.reshape(batch * s_len, D_MODEL)
    return x2d.reshape(batch, s_len, D_MODEL)
```

```python
import functools

import jax
import jax.numpy as jnp
from jax import lax
from jax.experimental import pallas as pl
from jax.experimental.pallas import tpu as pltpu

F32 = jnp.float32
BF16 = jnp.bfloat16

LANES = 128
D_MODEL = 1024
N_HEADS = 4
HEAD_DIM = 128
MIX_WIDTH = N_HEADS * HEAD_DIM
DN_CONV = 5
D_FF = 2816
EPS = 1e-6
CHUNK = 64
SUB = 16
SUB_SHIFT = SUB.bit_length() - 1
HG_SUPER = 256
HG_PAIR = 2
HG_SCAN_UNROLL = 16
HG_SAFE_EXP = 60.0
CHUNK_SHIFT = CHUNK.bit_length() - 1
DN_SUPER = 256
DN_PAIR = 4
DN_SEQS = 2
ACT_DTYPE = BF16
VMEM_LIMIT = 56 * 1024 * 1024

N_MAIN = 2 * D_MODEL + 4 * MIX_WIDTH + 5 * MIX_WIDTH
T_GATE_DN, T_GATE_HG = 0, 1
T_Q, T_K, T_V, T_Z = 16, 20, 24, 28
T_HQ, T_HFF, T_HFB, T_HI, T_HGATE = 32, 36, 40, 44, 48
N_SMALL = 16

_C_Z_END = 4 * MIX_WIDTH
_C_HQ = _C_Z_END + N_SMALL
_C_GATE = _C_HQ + 5 * MIX_WIDTH


def _sigmoid(x):
    return 0.5 * jnp.tanh(0.5 * x) + 0.5


def _log1p_nonneg(x):
    return jnp.log(1.0 + x)


def _softplus(x):
    return jnp.maximum(x, 0.0) + _log1p_nonneg(jnp.exp(-jnp.abs(x)))


def _rms(x, w):
    return x * lax.rsqrt(jnp.mean(x * x, axis=-1, keepdims=True) + EPS) * w


def _dot(a, b):
    return jnp.dot(a, b, preferred_element_type=F32)


def _dot_nt(a, b):
    return lax.dot_general(a, b, (((1,), (1,)), ((), ())), preferred_element_type=F32)


def _dot_tn(a, b):
    return lax.dot_general(a, b, (((0,), (0,)), ((), ())), preferred_element_type=F32)


def _split3(x):
    hi = x.astype(BF16)
    r = x - hi.astype(F32)
    mid = r.astype(BF16)
    lo = (r - mid.astype(F32)).astype(BF16)
    return hi, mid, lo


def _sel_left(m16, x):
    w = x.shape[1]
    y = _dot(m16, jnp.concatenate(_split3(x), axis=1))
    return y[:, :w] + y[:, w:2 * w] + y[:, 2 * w:]


def _sel_right(x, m16):
    r = x.shape[0]
    y = _dot(jnp.concatenate(_split3(x), axis=0), m16)
    return y[:r] + y[r:2 * r] + y[2 * r:]


def _inproj_kernel(x_ref, nw_ref, w_ref, ws_ref, wst_ref, p_ref, s_ref, st_ref, h_ref):
    @pl.when(pl.program_id(1) == 0)
    def _():
        hb = _rms(x_ref[...], nw_ref[...]).astype(BF16)
        h_ref[...] = hb
        s_ref[...] = _dot(hb, ws_ref[...])
        st_ref[...] = _dot_nt(wst_ref[...], hb)

    p_ref[...] = _dot(h_ref[...], w_ref[...]).astype(p_ref.dtype)


def _inproj(x2d, norm_w, w_main, w_small, w_small_t):
    t = x2d.shape[0]
    tm = min(1024, t)
    tn = N_MAIN // 2
    return pl.pallas_call(
        _inproj_kernel,
        grid=(t // tm, N_MAIN // tn),
        in_specs=[
            pl.BlockSpec((tm, D_MODEL), lambda i, j: (i, 0)),
            pl.BlockSpec((1, D_MODEL), lambda i, j: (0, 0)),
            pl.BlockSpec((D_MODEL, tn), lambda i, j: (0, j)),
            pl.BlockSpec((D_MODEL, LANES), lambda i, j: (0, 0)),
            pl.BlockSpec((N_SMALL, D_MODEL), lambda i, j: (0, 0)),
        ],
        out_specs=[
            pl.BlockSpec((tm, tn), lambda i, j: (i, j)),
            pl.BlockSpec((tm, LANES), lambda i, j: (i, 0)),
            pl.BlockSpec((N_SMALL, tm), lambda i, j: (0, i)),
        ],
        out_shape=[
            jax.ShapeDtypeStruct((t, N_MAIN), ACT_DTYPE),
            jax.ShapeDtypeStruct((t, LANES), F32),
            jax.ShapeDtypeStruct((N_SMALL, t), F32),
        ],
        scratch_shapes=[pltpu.VMEM((tm, D_MODEL), BF16)],
        compiler_params=pltpu.CompilerParams(
            dimension_semantics=("arbitrary", "arbitrary"), vmem_limit_bytes=VMEM_LIMIT),
        name="inproj",
    )(x2d, norm_w, w_main, w_small, w_small_t)


def _neumann_inverses(mats, block):
    n = mats[0].shape[0]
    nb = n // block
    shift = block.bit_length() - 1
    steps = shift
    r_p = lax.broadcasted_iota(jnp.int32, (block, n), 0)
    c_p = lax.broadcasted_iota(jnp.int32, (block, n), 1)
    c_blk = lax.shift_right_logical(c_p, shift)
    r_n = lax.broadcasted_iota(jnp.int32, (n, n), 0)
    c_n = lax.broadcasted_iota(jnp.int32, (n, n), 1)
    same = lax.shift_right_logical(r_n, shift) == lax.shift_right_logical(c_n, shift)

    def pack(m):
        out = m[0:block]
        for j in range(1, nb):
            out = jnp.where(c_blk == j, m[j * block:(j + 1) * block], out)
        return out

    def unpack(p):
        return jnp.where(same, jnp.concatenate([p] * nb, axis=0), 0.0)

    eye_p = (r_p == (c_p & (block - 1))).astype(F32)
    xs = [pack(a) for a in mats]
    ss = [eye_p - x for x in xs]
    xs = [_dot(x.astype(BF16), a.astype(BF16)) for x, a in zip(xs, mats)]
    for k in range(1, steps):
        x_bd = [unpack(x).astype(BF16) for x in xs]
        if k < steps - 1:
            lhs = [jnp.concatenate([s, x], axis=0).astype(BF16) for s, x in zip(ss, xs)]
            both = [_dot(l, xb) for l, xb in zip(lhs, x_bd)]
            ss = [s + b[:block] for s, b in zip(ss, both)]
            xs = [b[block:] for b in both]
        else:
            s16 = [s.astype(BF16) for s in ss]
            ss = [s + _dot(l, xb) for s, l, xb in zip(ss, s16, x_bd)]
    return [unpack(s) for s in ss]


def _dn_kernel(seqs, alog_ref, dtb_ref, q_ref, k_ref, v_ref, z_ref, small_ref, smallt_ref,
               cq_ref, ck_ref, cv_ref, nw_ref, o_ref,
               qpad, kpad, vpad, stg, ball, gcall, grow, st_scr, of_scr, mt_scr, nt_scr, qh_scr,
               gl_scr):
    h = pl.program_id(1)
    s_len = q_ref.shape[0] // seqs
    nc = s_len // CHUNK
    lane = lax.broadcasted_iota(jnp.int32, (1, LANES), 1)

    for pad, x_ref in ((qpad, q_ref), (kpad, k_ref), (vpad, v_ref)):
        for s in range(seqs):
            pad[s, 0:8, :] = jnp.zeros((8, LANES), F32)
            pad[s, 8 + s_len:16 + s_len, :] = jnp.zeros((8, LANES), F32)
            pad[s, 8:8 + s_len, :] = x_ref[s * s_len:(s + 1) * s_len, :].astype(F32)

    def conv_silu(pad, cw_ref, seq, base):
        win = pad[seq, pl.ds(base, DN_SUPER + 16), :]
        y = cw_ref[0:1, :] * win[6:6 + DN_SUPER, :]
        for j in range(1, DN_CONV):
            y = y + cw_ref[j:j + 1, :] * win[6 + j:6 + j + DN_SUPER, :]
        return y * _sigmoid(y)

    def l2n(y):
        return y * lax.rsqrt(jnp.sum(y * y, axis=-1, keepdims=True) + EPS)

    n_sup = s_len // DN_SUPER
    n_tot = seqs * n_sup
    per_sup = DN_SUPER // CHUNK
    r_i = lax.broadcasted_iota(jnp.int32, (DN_SUPER, DN_SUPER), 0)
    c_i = lax.broadcasted_iota(jnp.int32, (DN_SUPER, DN_SUPER), 1)
    same = lax.shift_right_logical(r_i, CHUNK_SHIFT) == lax.shift_right_logical(c_i, CHUNK_SHIFT)
    incl = (same & (r_i >= c_i), same & (r_i <= c_i))
    strict = (same & (r_i > c_i), same & (r_i < c_i))

    @pl.when(h == 0)
    def _():
        alv = jnp.zeros((1, LANES), F32)
        dtv = jnp.zeros((1, LANES), F32)
        for d in range(2):
            for hh in range(N_HEADS):
                alv = jnp.where(lane == 8 + d * N_HEADS + hh, alog_ref[d, hh], alv)
                dtv = jnp.where(lane == 8 + d * N_HEADS + hh, dtb_ref[d, hh], dtv)
        sm = small_ref[...]
        ball[...] = _sigmoid(sm)
        gcall[...] = -jnp.exp(alv) * _softplus(sm + dtv)
        tri_both = jnp.concatenate([incl[0].astype(BF16), incl[1].astype(BF16)], axis=0)
        fwd_lane = lane < 8 + N_HEADS

        def cumulate(g, carry):
            rows = pl.ds(pl.multiple_of(g * DN_SUPER, DN_SUPER), DN_SUPER)
            both = _sel_left(tri_both, gcall[rows, :])
            gcall[rows, :] = jnp.where(fwd_lane, both[:DN_SUPER], both[DN_SUPER:])
            return carry

        lax.fori_loop(0, n_tot, cumulate, 0)

    def pick(x, idx):
        col = jnp.sum(jnp.where(lane == idx, x, 0.0), axis=-1, keepdims=True)
        return jnp.broadcast_to(col, x.shape)

    for d in range(2):
        al = jnp.full((1, DN_SUPER), alog_ref[d, h], F32)
        dt = jnp.full((1, DN_SUPER), dtb_ref[d, h], F32)
        for s in range(seqs):
            g_r = -jnp.exp(al) * _softplus(smallt_ref[s, 8 + d * N_HEADS + h] + dt)
            grow[d, s * n_sup:(s + 1) * n_sup, :] = _sel_right(g_r, incl[1 - d].astype(BF16))

    row_blk = lax.shift_right_logical(lax.broadcasted_iota(jnp.int32, (DN_SUPER, 1), 0), CHUNK_SHIFT)

    def stage(g, p, j):
        seq = lax.div(g, n_sup)
        base = pl.multiple_of(lax.rem(g, n_sup) * DN_SUPER, DN_SUPER)
        rows = pl.ds(pl.multiple_of(g * DN_SUPER, DN_SUPER), DN_SUPER)
        stg[p, j, 0] = l2n(conv_silu(qpad, cq_ref, seq, base)) * (HEAD_DIM ** -0.5)
        stg[p, j, 1] = l2n(conv_silu(kpad, ck_ref, seq, base))
        stg[p, j, 2] = conv_silu(vpad, cv_ref, seq, base)
        for d in range(2):
            stg[p, j, 3 + 2 * d] = pick(gcall[rows, :], 8 + d * N_HEADS + h)
            stg[p, j, 4 + 2 * d] = pick(ball[rows, :], d * N_HEADS + h)

    n_pairs = n_tot // DN_PAIR
    chains = [(j, d) for j in range(DN_PAIR) for d in range(2)]

    def prepare_pair(gg, carry):
        p = gg & 1
        nxt = jnp.minimum(gg + 1, n_pairs - 1)
        for j in range(DN_PAIR):
            stage(nxt * DN_PAIR + j, 1 - p, j)
        q = [stg[p, j, 0] for j in range(DN_PAIR)]
        k = [stg[p, j, 1] for j in range(DN_PAIR)]
        v = [stg[p, j, 2] for j in range(DN_PAIR)]
        k16 = [x.astype(BF16) for x in k]
        kk = [_dot_nt(x, x) for x in k16]
        qk_raw = [_dot_nt(x.astype(BF16), y) for x, y in zip(q, k16)]
        gc_c, be_c, decay, a_mats = {}, {}, {}, []
        for j, d in chains:
            gc_c[j, d] = stg[p, j, 3 + 2 * d]
            be_c[j, d] = stg[p, j, 4 + 2 * d]
            gc_r = grow[d, pl.ds(gg * DN_PAIR + j, 1), :]
            diff = jnp.concatenate([gc_c[j, d]] * 2, axis=1) - gc_r
            decay[j, d] = jnp.where(incl[d], jnp.exp(jnp.where(incl[d], diff, 0.0)), 0.0)
            a_mats.append(jnp.where(strict[d], kk[j] * jnp.concatenate([be_c[j, d]] * 2, axis=1) * decay[j, d], 0.0))
        t_inv = _neumann_inverses(a_mats, CHUNK)
        e_gc = {c: jnp.exp(gc_c[c]) for c in chains}
        rhs = [jnp.concatenate([v[j] * be_c[j, d], k[j] * (be_c[j, d] * e_gc[j, d])], axis=1).astype(BF16)
               for j, d in chains]
        uw16 = [_dot(t.astype(BF16), r).astype(BF16) for t, r in zip(t_inv, rhs)]
        qk16 = [(qk_raw[j] * decay[j, d]).astype(BF16) for j, d in chains]
        qu_qw = [_dot(x, y) for x, y in zip(qk16, uw16)]
        gc_last, kd_bd = [], []
        for j, d in chains:
            last = CHUNK - 1 if d == 0 else 0
            gl = [gc_c[j, d][i * CHUNK + last:i * CHUNK + last + 1, :] for i in range(per_sup)]
            gcl = jnp.concatenate([jnp.broadcast_to(x, (CHUNK, LANES)) for x in gl], axis=0)
            kd = k[j] * jnp.exp(gcl - gc_c[j, d])
            gc_last.append(gl)
            kd_bd.append(jnp.concatenate(
                [jnp.where(row_blk == i, kd, 0.0) for i in range(per_sup)], axis=1).astype(BF16))
        mn = [_dot_tn(x, y) for x, y in zip(uw16, kd_bd)]
        for n, (j, d) in enumerate(chains):
            g = gg * DN_PAIR + j
            rows = pl.ds(pl.multiple_of(g * DN_SUPER, DN_SUPER), DN_SUPER)
            for i in range(per_sup):
                c = g * per_sup + i
                cols = slice(i * HEAD_DIM, (i + 1) * HEAD_DIM)
                nt_scr[d, c] = mn[n][:HEAD_DIM, cols]
                mt_scr[d, c] = (-mn[n][HEAD_DIM:, cols]).astype(BF16)
                gl_scr[d, c] = jnp.exp(gc_last[n][i])
            qh_scr[d, rows, :] = (q[j] * e_gc[j, d] - qu_qw[n][:, HEAD_DIM:]).astype(BF16)
            if d == 1:
                of_scr[rows, :] = qu_qw[n - 1][:, :HEAD_DIM] + qu_qw[n][:, :HEAD_DIM]
        return carry

    for j in range(DN_PAIR):
        stage(j, 0, j)
    lax.fori_loop(0, n_pairs, prepare_pair, 0)

    st_scr[...] = jnp.zeros(st_scr.shape, F32)

    def scan_step(i, carry):
        for s in range(seqs):
            for d in range(2):
                c = s * nc + (i if d == 0 else nc - 1 - i)
                rows = pl.ds(pl.multiple_of(c * CHUNK, CHUNK), CHUNK)
                st = st_scr[s, d]
                st16 = st.astype(BF16)
                of_scr[rows, :] += _dot_nt(qh_scr[d, rows, :], st16)
                st_scr[s, d] = st * gl_scr[d, c] + _dot(st16, mt_scr[d, c]) + nt_scr[d, c]
        return carry

    lax.fori_loop(0, nc, scan_step, 0)

    o = of_scr[...]
    y = o * lax.rsqrt(jnp.mean(o * o, axis=-1, keepdims=True) + EPS) * nw_ref[...]
    zz = z_ref[...].astype(F32)
    o_ref[...] = (y * (zz * _sigmoid(zz))).astype(o_ref.dtype)


def _deltanet(p_main, small, small_t, conv_w, a_log, dt_bias, norm_w, batch, s_len):
    seqs = DN_SEQS if batch % DN_SEQS == 0 else 1
    n_rows = seqs * s_len
    nc = n_rows // CHUNK
    n_sup = s_len // DN_SUPER
    assert s_len % DN_SUPER == 0 and (seqs * n_sup) % DN_PAIR == 0
    tok = lambda tile: pl.BlockSpec((n_rows, LANES), lambda b, h, tile=tile: (b, tile + h))
    cw = lambda tile: pl.BlockSpec((DN_CONV, LANES), lambda b, h, tile=tile: (0, tile + h))
    smem = pl.BlockSpec(memory_space=pltpu.SMEM)
    return pl.pallas_call(
        functools.partial(_dn_kernel, seqs),
        grid=(batch // seqs, N_HEADS),
        in_specs=[
            smem, smem,
            tok(T_Q), tok(T_K), tok(T_V), tok(T_Z),
            pl.BlockSpec((n_rows, LANES), lambda b, h: (b, 0)),
            pl.BlockSpec((seqs, N_SMALL, n_sup, DN_SUPER), lambda b, h: (b, 0, 0, 0)),
            cw(0), cw(N_HEADS), cw(2 * N_HEADS),
            pl.BlockSpec((1, LANES), lambda b, h: (0, 0)),
        ],
        out_specs=pl.BlockSpec((n_rows, LANES), lambda b, h: (b, h)),
        out_shape=jax.ShapeDtypeStruct((batch * s_len, MIX_WIDTH), ACT_DTYPE),
        scratch_shapes=[
            pltpu.VMEM((seqs, s_len + 16, LANES), F32),
            pltpu.VMEM((seqs, s_len + 16, LANES), F32),
            pltpu.VMEM((seqs, s_len + 16, LANES), F32),
            pltpu.VMEM((2, DN_PAIR, 7, DN_SUPER, LANES), F32),
            pltpu.VMEM((n_rows, LANES), F32),
            pltpu.VMEM((n_rows, LANES), F32),
            pltpu.VMEM((2, seqs * n_sup, DN_SUPER), F32),
            pltpu.VMEM((seqs, 2, HEAD_DIM, HEAD_DIM), F32),
            pltpu.VMEM((n_rows, LANES), F32),
            pltpu.VMEM((2, nc, HEAD_DIM, HEAD_DIM), BF16),
            pltpu.VMEM((2, nc, HEAD_DIM, HEAD_DIM), F32),
            pltpu.VMEM((2, n_rows, LANES), BF16),
            pltpu.VMEM((2, nc, 1, LANES), F32),
        ],
        compiler_params=pltpu.CompilerParams(
            dimension_semantics=("arbitrary", "arbitrary"), vmem_limit_bytes=VMEM_LIMIT),
        name="deltanet",
    )(a_log, dt_bias, p_main, p_main, p_main, p_main, small, small_t,
      conv_w, conv_w, conv_w, norm_w)


def _hg_kernel(layer, hq_ref, hff_ref, hfb_ref, hi_ref, hg_ref, lbl_ref, nw_ref, o_ref,
               qs, iv, lf, kk, st_scr, oi_scr, nt_scr, fl_scr, qt_scr):
    s_len = hq_ref.shape[0]
    nc = s_len // CHUNK

    lg = lbl_ref[...].astype(F32)
    e = jnp.exp(lg - jnp.max(lg, axis=0, keepdims=True))
    sm = e / jnp.sum(e, axis=0, keepdims=True)
    lb = jnp.clip(jnp.sum(sm[:layer + 1], axis=0) - sm[0], 0.0, 1.0)

    x = hq_ref[...].astype(F32)
    qs[...] = (x * _sigmoid(x)) * (HEAD_DIM ** -0.5)
    iv[...] = hi_ref[...].astype(BF16)
    for d, ref in enumerate((hff_ref, hfb_ref)):
        fr = ref[...].astype(F32)
        lbd = lb[d:d + 1, :]
        e = jnp.exp(-jnp.abs(fr))
        lf[d] = jnp.log(jnp.where(fr >= 0.0, 1.0 + lbd * e, e + lbd)) - jnp.log(1.0 + e)
        kk[d] = (1.0 - lbd) * _sigmoid(-fr)

    n_sup = s_len // HG_SUPER
    per_sup = HG_SUPER // CHUNK
    subs = CHUNK // SUB
    r_i = lax.broadcasted_iota(jnp.int32, (HG_SUPER, HG_SUPER), 0)
    c_i = lax.broadcasted_iota(jnp.int32, (HG_SUPER, HG_SUPER), 1)
    same_chunk = lax.shift_right_logical(r_i, CHUNK_SHIFT) == lax.shift_right_logical(c_i, CHUNK_SHIFT)
    r_sub = lax.shift_right_logical(r_i, SUB_SHIFT)
    c_sub = lax.shift_right_logical(c_i, SUB_SHIFT)
    causal = (r_i >= c_i, r_i <= c_i)
    row_i = lax.broadcasted_iota(jnp.int32, (HG_SUPER, 1), 0)
    row_sub = lax.shift_right_logical(row_i, SUB_SHIFT) & (subs - 1)
    row_blk = lax.shift_right_logical(row_i, CHUNK_SHIFT)
    pos16 = lax.broadcasted_iota(jnp.int32, (1, SUB, 1), 1)

    def group_row(x, group, idx):
        x3 = x.reshape(x.shape[0] // group, group, LANES)
        return jnp.broadcast_to(x3[:, idx:idx + 1, :], x3.shape).reshape(x.shape)

    chains = [(j, d) for j in range(HG_PAIR) for d in range(2)]
    tri = [(same_chunk & causal[d]).astype(BF16) for d in range(2)]
    diag_mask = [(r_sub == c_sub) & causal[d] for d in range(2)]

    def prepare_pair(gg, carry):
        sup = [gg * HG_PAIR + j for j in range(HG_PAIR)]
        rows = [pl.ds(pl.multiple_of(g * HG_SUPER, HG_SUPER), HG_SUPER) for g in sup]
        q = [qs[r, :] for r in rows]
        i16 = [iv[r, :] for r in rows]
        k = {(j, d): kk[d, rows[j], :] for j, d in chains}
        b = {(j, d): _sel_left(tri[d], lf[d, rows[j], :]) for j, d in chains}
        qh, keys, masks, expo = {}, {}, {}, {}
        for c in chains:
            j, d = c
            first = 0 if d == 0 else SUB - 1
            ref16 = group_row(b[c], SUB, first)
            expo[c] = ref16 - b[c]
            qh[c] = (q[j] * jnp.exp(b[c] - ref16)).astype(BF16)
            keys[c] = [(k[c] * jnp.exp(jnp.minimum(expo[c], HG_SAFE_EXP))).astype(BF16)]
            masks[c] = []
            for t_sub in (range(1, subs) if d == 0 else range(subs - 1)):
                ref_t = group_row(b[c], CHUNK, t_sub * SUB + first)
                src = (row_sub < t_sub) if d == 0 else (row_sub > t_sub)
                keys[c].append(
                    jnp.where(src, k[c] * jnp.exp(jnp.where(src, ref_t - b[c], 0.0)), 0.0).astype(BF16))
                masks[c].append(same_chunk & ((r_sub & (subs - 1)) == t_sub) & (
                    ((c_sub & (subs - 1)) < t_sub) if d == 0 else ((c_sub & (subs - 1)) > t_sub)))
        att_all = {c: _dot_nt(qh[c], jnp.concatenate(keys[c], axis=0)) for c in chains}
        att_off, att16 = {}, {}
        for c in chains:
            off = jnp.zeros((HG_SUPER, HG_SUPER), F32)
            for n, m in enumerate(masks[c]):
                off = off + jnp.where(m, att_all[c][:, (n + 1) * HG_SUPER:(n + 2) * HG_SUPER], 0.0)
            att_off[c] = off
            att16[c] = (off + jnp.where(diag_mask[c[1]], att_all[c][:, :HG_SUPER], 0.0)).astype(BF16)
        o_fast = {c: _dot(att16[c], i16[c[0]]) for c in chains}
        b_last, kt_bd = {}, {}
        for c in chains:
            j, d = c
            oi_scr[d, rows[j], :] = o_fast[c]
            b_last[c] = group_row(b[c], CHUNK, CHUNK - 1 if d == 0 else 0)
            qt_scr[d, rows[j], :] = (q[j] * jnp.exp(b[c])).astype(BF16)
            kt = k[c] * jnp.exp(b_last[c] - b[c])
            kt_bd[c] = jnp.concatenate(
                [jnp.where(row_blk == n, kt, 0.0) for n in range(per_sup)], axis=1).astype(BF16)
        nts = {c: _dot_tn(i16[c[0]], kt_bd[c]) for c in chains}
        for c in chains:
            j, d = c
            for n in range(per_sup):
                cc = sup[j] * per_sup + n
                nt_scr[d, cc] = nts[c][:, n * HEAD_DIM:(n + 1) * HEAD_DIM]
                fl_scr[d, cc] = jnp.exp(b_last[c][n * CHUNK:n * CHUNK + 1, :])

        for c in chains:
            j, d = c

            @pl.when(jnp.max(expo[c]) > HG_SAFE_EXP)
            def _(c=c, j=j, d=d):
                b3 = b[c].reshape(HG_SUPER // SUB, SUB, LANES)
                q3 = q[j].reshape(b3.shape)
                k3 = k[c].reshape(b3.shape)
                i3 = i16[j].astype(F32).reshape(b3.shape)
                acc = jnp.zeros(b3.shape, F32)
                for s in range(SUB):
                    ok = (pos16 >= s) if d == 0 else (pos16 <= s)
                    ex = jnp.exp(jnp.where(ok, b3 - b3[:, s:s + 1, :], 0.0))
                    col = jnp.sum(q3 * k3[:, s:s + 1, :] * ex, axis=-1, keepdims=True)
                    acc = acc + jnp.where(ok, col, 0.0) * i3[:, s:s + 1, :]
                oi_scr[d, rows[j], :] = (_dot(att_off[c].astype(BF16), i16[j])
                                         + acc.reshape(HG_SUPER, LANES))
        return carry

    lax.fori_loop(0, n_sup // HG_PAIR, prepare_pair, 0)

    st_scr[...] = jnp.zeros(st_scr.shape, F32)

    unroll = min(HG_SCAN_UNROLL, nc)
    assert nc % unroll == 0

    def scan_step(ii, carry):
        for u in range(unroll):
            i = ii * unroll + u
            for d in range(2):
                c = i if d == 0 else nc - 1 - i
                rows = pl.ds(pl.multiple_of(c * CHUNK, CHUNK), CHUNK)
                st = st_scr[d]
                oi_scr[d, rows, :] += _dot_nt(qt_scr[d, rows, :], st.astype(BF16))
                st_scr[d] = st * fl_scr[d, c] + nt_scr[d, c]
        return carry

    lax.fori_loop(0, nc // unroll, scan_step, 0)

    o = oi_scr[0] + oi_scr[1]
    y = o * lax.rsqrt(jnp.mean(o * o, axis=-1, keepdims=True) + EPS) * nw_ref[...]
    zz = hg_ref[...].astype(F32)
    o_ref[...] = (y * (zz * _sigmoid(zz))).astype(o_ref.dtype)


def _hgrn2(p_main, lb_logits, norm_w, layer, batch, s_len):
    nc = s_len // CHUNK
    assert s_len % HG_SUPER == 0
    depth = lb_logits.shape[0]
    tok = lambda tile: pl.BlockSpec((s_len, LANES), lambda b, h, tile=tile: (b, tile + h))
    return pl.pallas_call(
        functools.partial(_hg_kernel, layer),
        grid=(batch, N_HEADS),
        in_specs=[
            tok(T_HQ), tok(T_HFF), tok(T_HFB), tok(T_HI), tok(T_HGATE),
            pl.BlockSpec((depth, 2, LANES), lambda b, h: (0, 0, h)),
            pl.BlockSpec((1, LANES), lambda b, h: (0, 0)),
        ],
        out_specs=pl.BlockSpec((s_len, LANES), lambda b, h: (b, h)),
        out_shape=jax.ShapeDtypeStruct((batch * s_len, MIX_WIDTH), ACT_DTYPE),
        scratch_shapes=[
            pltpu.VMEM((s_len, LANES), F32),
            pltpu.VMEM((s_len, LANES), BF16),
            pltpu.VMEM((2, s_len, LANES), F32),
            pltpu.VMEM((2, s_len, LANES), F32),
            pltpu.VMEM((2, HEAD_DIM, HEAD_DIM), F32),
            pltpu.VMEM((2, s_len, LANES), F32),
            pltpu.VMEM((2, nc, HEAD_DIM, HEAD_DIM), F32),
            pltpu.VMEM((2, nc, 1, LANES), F32),
            pltpu.VMEM((2, s_len, LANES), BF16),
        ],
        compiler_params=pltpu.CompilerParams(
            dimension_semantics=("arbitrary", "arbitrary"), vmem_limit_bytes=VMEM_LIMIT),
        name="hgrn2",
    )(p_main, p_main, p_main, p_main, p_main, lb_logits, norm_w)


def _merge_kernel(x_ref, odn_ref, ohg_ref, gdn_ref, ghg_ref, wdn_ref, whg_ref, wo_ref, o_ref):
    a = _dot(odn_ref[...].astype(BF16), wdn_ref[...])
    b = _dot(ohg_ref[...].astype(BF16), whg_ref[...])
    merged = _sigmoid(gdn_ref[...].astype(F32)) * a + _sigmoid(ghg_ref[...].astype(F32)) * b
    o_ref[...] = x_ref[...] + _dot(merged.astype(BF16), wo_ref[...])


def _merge_out(x2d, o_dn, o_hg, p_main, w_dn, w_hg, w_out):
    t = x2d.shape[0]
    tm = min(1024, t)
    const = lambda shape: pl.BlockSpec(shape, lambda i: (0, 0))
    return pl.pallas_call(
        _merge_kernel,
        grid=(t // tm,),
        in_specs=[
            pl.BlockSpec((tm, D_MODEL), lambda i: (i, 0)),
            pl.BlockSpec((tm, MIX_WIDTH), lambda i: (i, 0)),
            pl.BlockSpec((tm, MIX_WIDTH), lambda i: (i, 0)),
            pl.BlockSpec((tm, D_MODEL), lambda i: (i, T_GATE_DN)),
            pl.BlockSpec((tm, D_MODEL), lambda i: (i, T_GATE_HG)),
            const((MIX_WIDTH, D_MODEL)), const((MIX_WIDTH, D_MODEL)), const((D_MODEL, D_MODEL)),
        ],
        out_specs=pl.BlockSpec((tm, D_MODEL), lambda i: (i, 0)),
        out_shape=jax.ShapeDtypeStruct((t, D_MODEL), F32),
        compiler_params=pltpu.CompilerParams(
            dimension_semantics=("arbitrary",), vmem_limit_bytes=VMEM_LIMIT),
        name="merge_out",
    )(x2d, o_dn, o_hg, p_main, p_main, w_dn, w_hg, w_out)


FF_COLS = D_FF // 2


def _ffn_kernel(final, xm_ref, xp_ref, xn_ref, nw_ref, wg_ref, wu_ref, cw_ref, cb_ref, wd_ref, fn_ref,
                o_ref, g_scr):
    i = pl.program_id(1)
    last = pl.num_programs(1) - 1
    ts = xm_ref.shape[1]
    x = xm_ref[0]
    h_ext = _rms(jnp.concatenate([xp_ref[0], x, xn_ref[0]], axis=0), nw_ref[...])
    h = h_ext[8:8 + ts].astype(BF16)
    h_ext = h_ext.astype(BF16)
    acc = x
    for c in range(D_FF // FF_COLS):
        cols = slice(c * FF_COLS, (c + 1) * FF_COLS)
        g_scr[...] = _dot(h_ext, wg_ref[:, cols])
        g_scr[7:8, :] = jnp.where(i > 0, g_scr[7:8, :], 0.0)
        g_scr[8 + ts:9 + ts, :] = jnp.where(i < last, g_scr[8 + ts:9 + ts, :], 0.0)
        gc = (cw_ref[0:1, cols] * g_scr[7:7 + ts, :] + cw_ref[1:2, cols] * g_scr[8:8 + ts, :]
              + cw_ref[2:3, cols] * g_scr[9:9 + ts, :] + cb_ref[:, cols])
        u = _dot(h, wu_ref[:, cols])
        act = (gc * _sigmoid(gc)) * u
        acc = acc + _dot(act.astype(BF16), wd_ref[cols, :])
    if final:
        acc = _rms(acc, fn_ref[...])
    o_ref[0] = acc


def _ffn(x3d, norm_w, w_gate, w_upp, conv_w, conv_b, w_down, final_w, final):
    batch, s_len, _ = x3d.shape
    ts = min(512, s_len)
    r8 = ts // 8
    n8 = s_len // 8
    const = lambda shape: pl.BlockSpec(shape, lambda b, i: (0, 0), pipeline_mode=pl.Buffered(1))
    return pl.pallas_call(
        functools.partial(_ffn_kernel, final),
        grid=(batch, s_len // ts),
        in_specs=[
            pl.BlockSpec((1, ts, D_MODEL), lambda b, i: (b, i, 0)),
            pl.BlockSpec((1, 8, D_MODEL), lambda b, i: (b, jnp.maximum(i * r8 - 1, 0), 0)),
            pl.BlockSpec((1, 8, D_MODEL), lambda b, i: (b, jnp.minimum((i + 1) * r8, n8 - 1), 0)),
            const((1, D_MODEL)),
            const((D_MODEL, D_FF)), const((D_MODEL, D_FF)),
            const((3, D_FF)), const((1, D_FF)),
            const((D_FF, D_MODEL)),
            const((1, D_MODEL)),
        ],
        out_specs=pl.BlockSpec((1, ts, D_MODEL), lambda b, i: (b, i, 0)),
        out_shape=jax.ShapeDtypeStruct(x3d.shape, F32),
        scratch_shapes=[pltpu.VMEM((ts + 16, FF_COLS), F32)],
        compiler_params=pltpu.CompilerParams(
            dimension_semantics=("arbitrary", "arbitrary"), vmem_limit_bytes=VMEM_LIMIT),
        name="ffn",
    )(x3d, x3d, x3d, norm_w, w_gate, w_upp, conv_w, conv_b, w_down, final_w)


def kernel(x, mix_norm, w_in, dn_conv, dn_a_log, dn_dt_bias, dn_norm, hg_lb_logits, hg_norm,
           w_branch_dn, w_branch_hg, w_out, ffn_norm, w_up, ffn_conv, ffn_conv_bias, w_down, final_norm):
    batch, s_len, _ = x.shape
    depth = mix_norm.shape[0]
    x2d = x.reshape(batch * s_len, D_MODEL)
    final_w = final_norm.reshape(1, D_MODEL)
    for l in range(depth):
        w = w_in[l]
        w_main = jnp.concatenate(
            [w[:, _C_GATE:], w[:, :_C_Z_END], w[:, _C_HQ:_C_GATE]], axis=1).astype(BF16)
        w_s = w[:, _C_Z_END:_C_HQ].astype(BF16)
        w_small = jnp.pad(w_s, ((0, 0), (0, LANES - N_SMALL)))
        p_main, small, small_t = _inproj(x2d, mix_norm[l].reshape(1, D_MODEL), w_main, w_small, w_s.T)
        small_t = small_t.reshape(N_SMALL, batch, s_len // DN_SUPER, DN_SUPER).transpose(1, 0, 2, 3)
        o_dn = _deltanet(p_main, small, small_t, dn_conv[l], dn_a_log[l], dn_dt_bias[l],
                         dn_norm[l].reshape(1, HEAD_DIM), batch, s_len)
        o_hg = _hgrn2(p_main, hg_lb_logits, hg_norm[l].reshape(1, HEAD_DIM), l, batch, s_len)
        x2d = _merge_out(x2d, o_dn, o_hg, p_main, w_branch_dn[l].astype(BF16),
                         w_branch_hg[l].astype(BF16), w_out[l].astype(BF16))
        wu = w_up[l].astype(BF16)
        x3d = _ffn(x2d.reshape(batch, s_len, D_MODEL), ffn_norm[l].reshape(1, D_MODEL),
                   wu[:, :D_FF], wu[:, D_FF:], ffn_conv[l], ffn_conv_bias[l].reshape(1, D_FF),
                   w_down[l].astype(BF16), final_w, l == depth - 1)
        x2d = x3d.reshape(batch * s_len, D_MODEL)
    return x2d.reshape(batch, s_len, D_MODEL)
```

```python
import functools

import jax
import jax.numpy as jnp
from jax import lax
from jax.experimental import pallas as pl
from jax.experimental.pallas import tpu as pltpu

F32 = jnp.float32
BF16 = jnp.bfloat16

LANES = 128
D_MODEL = 1024
N_HEADS = 4
HEAD_DIM = 128
MIX_WIDTH = N_HEADS * HEAD_DIM
DN_CONV = 5
D_FF = 2816
EPS = 1e-6
CHUNK = 64
SUB = 16
SUB_SHIFT = SUB.bit_length() - 1
HG_SUPER = 256
HG_PAIR = 2
HG_SCAN_UNROLL = 16
HG_SAFE_EXP = 60.0
CHUNK_SHIFT = CHUNK.bit_length() - 1
DN_SUPER = 256
DN_PAIR = 4
DN_SEQS = 2
ACT_DTYPE = BF16
VMEM_LIMIT = 56 * 1024 * 1024

N_MAIN = 2 * D_MODEL + 4 * MIX_WIDTH + 5 * MIX_WIDTH
T_GATE_DN, T_GATE_HG = 0, 1
T_Q, T_K, T_V, T_Z = 16, 20, 24, 28
T_HQ, T_HFF, T_HFB, T_HI, T_HGATE = 32, 36, 40, 44, 48
N_SMALL = 16

_C_Z_END = 4 * MIX_WIDTH
_C_HQ = _C_Z_END + N_SMALL
_C_GATE = _C_HQ + 5 * MIX_WIDTH


def _sigmoid(x):
    return 0.5 * jnp.tanh(0.5 * x) + 0.5


def _log1p_nonneg(x):
    return jnp.log(1.0 + x)


def _softplus(x):
    return jnp.maximum(x, 0.0) + _log1p_nonneg(jnp.exp(-jnp.abs(x)))


def _rms(x, w):
    return x * lax.rsqrt(jnp.mean(x * x, axis=-1, keepdims=True) + EPS) * w


def _dot(a, b):
    return jnp.dot(a, b, preferred_element_type=F32)


def _dot_nt(a, b):
    return lax.dot_general(a, b, (((1,), (1,)), ((), ())), preferred_element_type=F32)


def _dot_tn(a, b):
    return lax.dot_general(a, b, (((0,), (0,)), ((), ())), preferred_element_type=F32)


def _split3(x):
    hi = x.astype(BF16)
    r = x - hi.astype(F32)
    mid = r.astype(BF16)
    lo = (r - mid.astype(F32)).astype(BF16)
    return hi, mid, lo


def _sel_left(m16, x):
    w = x.shape[1]
    y = _dot(m16, jnp.concatenate(_split3(x), axis=1))
    return y[:, :w] + y[:, w:2 * w] + y[:, 2 * w:]


def _sel_right(x, m16):
    r = x.shape[0]
    y = _dot(jnp.concatenate(_split3(x), axis=0), m16)
    return y[:r] + y[r:2 * r] + y[2 * r:]


def _inproj_kernel(x_ref, nw_ref, w_ref, ws_ref, wst_ref, p_ref, s_ref, st_ref, h_ref):
    @pl.when(pl.program_id(1) == 0)
    def _():
        hb = _rms(x_ref[...], nw_ref[...]).astype(BF16)
        h_ref[...] = hb
        s_ref[...] = _dot(hb, ws_ref[...])
        st_ref[...] = _dot_nt(wst_ref[...], hb)

    p_ref[...] = _dot(h_ref[...], w_ref[...]).astype(p_ref.dtype)


def _inproj(x2d, norm_w, w_main, w_small, w_small_t):
    t = x2d.shape[0]
    tm = min(1024, t)
    tn = N_MAIN // 2
    return pl.pallas_call(
        _inproj_kernel,
        grid=(t // tm, N_MAIN // tn),
        in_specs=[
            pl.BlockSpec((tm, D_MODEL), lambda i, j: (i, 0)),
            pl.BlockSpec((1, D_MODEL), lambda i, j: (0, 0)),
            pl.BlockSpec((D_MODEL, tn), lambda i, j: (0, j)),
            pl.BlockSpec((D_MODEL, LANES), lambda i, j: (0, 0)),
            pl.BlockSpec((N_SMALL, D_MODEL), lambda i, j: (0, 0)),
        ],
        out_specs=[
            pl.BlockSpec((tm, tn), lambda i, j: (i, j)),
            pl.BlockSpec((tm, LANES), lambda i, j: (i, 0)),
            pl.BlockSpec((N_SMALL, tm), lambda i, j: (0, i)),
        ],
        out_shape=[
            jax.ShapeDtypeStruct((t, N_MAIN), ACT_DTYPE),
            jax.ShapeDtypeStruct((t, LANES), F32),
            jax.ShapeDtypeStruct((N_SMALL, t), F32),
        ],
        scratch_shapes=[pltpu.VMEM((tm, D_MODEL), BF16)],
        compiler_params=pltpu.CompilerParams(
            dimension_semantics=("arbitrary", "arbitrary"), vmem_limit_bytes=VMEM_LIMIT),
        name="inproj",
    )(x2d, norm_w, w_main, w_small, w_small_t)


def _neumann_inverses(mats, block):
    n = mats[0].shape[0]
    nb = n // block
    shift = block.bit_length() - 1
    steps = shift
    r_p = lax.broadcasted_iota(jnp.int32, (block, n), 0)
    c_p = lax.broadcasted_iota(jnp.int32, (block, n), 1)
    c_blk = lax.shift_right_logical(c_p, shift)
    r_n = lax.broadcasted_iota(jnp.int32, (n, n), 0)
    c_n = lax.broadcasted_iota(jnp.int32, (n, n), 1)
    same = lax.shift_right_logical(r_n, shift) == lax.shift_right_logical(c_n, shift)

    def pack(m):
        out = m[0:block]
        for j in range(1, nb):
            out = jnp.where(c_blk == j, m[j * block:(j + 1) * block], out)
        return out

    def unpack(p):
        return jnp.where(same, jnp.concatenate([p] * nb, axis=0), 0.0)

    eye_p = (r_p == (c_p & (block - 1))).astype(F32)
    xs = [pack(a) for a in mats]
    ss = [eye_p - x for x in xs]
    xs = [_dot(x.astype(BF16), a.astype(BF16)) for x, a in zip(xs, mats)]
    for k in range(1, steps):
        x_bd = [unpack(x).astype(BF16) for x in xs]
        if k < steps - 1:
            lhs = [jnp.concatenate([s, x], axis=0).astype(BF16) for s, x in zip(ss, xs)]
            both = [_dot(l, xb) for l, xb in zip(lhs, x_bd)]
            ss = [s + b[:block] for s, b in zip(ss, both)]
            xs = [b[block:] for b in both]
        else:
            s16 = [s.astype(BF16) for s in ss]
            ss = [s + _dot(l, xb) for s, l, xb in zip(ss, s16, x_bd)]
    return [unpack(s) for s in ss]


def _dn_kernel(seqs, alog_ref, dtb_ref, q_ref, k_ref, v_ref, z_ref, small_ref, smallt_ref,
               cq_ref, ck_ref, cv_ref, nw_ref, o_ref,
               qpad, kpad, vpad, stg, ball, gcall, grow, st_scr, of_scr, mt_scr, nt_scr, qh_scr,
               gl_scr):
    h = pl.program_id(1)
    s_len = q_ref.shape[0] // seqs
    nc = s_len // CHUNK
    lane = lax.broadcasted_iota(jnp.int32, (1, LANES), 1)

    for pad, x_ref in ((qpad, q_ref), (kpad, k_ref), (vpad, v_ref)):
        for s in range(seqs):
            pad[s, 0:8, :] = jnp.zeros((8, LANES), F32)
            pad[s, 8 + s_len:16 + s_len, :] = jnp.zeros((8, LANES), F32)
            pad[s, 8:8 + s_len, :] = x_ref[s * s_len:(s + 1) * s_len, :].astype(F32)

    def conv_silu(pad, cw_ref, seq, base):
        win = pad[seq, pl.ds(base, DN_SUPER + 16), :]
        y = cw_ref[0:1, :] * win[6:6 + DN_SUPER, :]
        for j in range(1, DN_CONV):
            y = y + cw_ref[j:j + 1, :] * win[6 + j:6 + j + DN_SUPER, :]
        return y * _sigmoid(y)

    def l2n(y):
        return y * lax.rsqrt(jnp.sum(y * y, axis=-1, keepdims=True) + EPS)

    n_sup = s_len // DN_SUPER
    n_tot = seqs * n_sup
    per_sup = DN_SUPER // CHUNK
    r_i = lax.broadcasted_iota(jnp.int32, (DN_SUPER, DN_SUPER), 0)
    c_i = lax.broadcasted_iota(jnp.int32, (DN_SUPER, DN_SUPER), 1)
    same = lax.shift_right_logical(r_i, CHUNK_SHIFT) == lax.shift_right_logical(c_i, CHUNK_SHIFT)
    incl = (same & (r_i >= c_i), same & (r_i <= c_i))
    strict = (same & (r_i > c_i), same & (r_i < c_i))

    @pl.when(h == 0)
    def _():
        alv = jnp.zeros((1, LANES), F32)
        dtv = jnp.zeros((1, LANES), F32)
        for d in range(2):
            for hh in range(N_HEADS):
                alv = jnp.where(lane == 8 + d * N_HEADS + hh, alog_ref[d, hh], alv)
                dtv = jnp.where(lane == 8 + d * N_HEADS + hh, dtb_ref[d, hh], dtv)
        sm = small_ref[...]
        ball[...] = _sigmoid(sm)
        gcall[...] = -jnp.exp(alv) * _softplus(sm + dtv)
        tri_both = jnp.concatenate([incl[0].astype(BF16), incl[1].astype(BF16)], axis=0)
        fwd_lane = lane < 8 + N_HEADS

        def cumulate(g, carry):
            rows = pl.ds(pl.multiple_of(g * DN_SUPER, DN_SUPER), DN_SUPER)
            both = _sel_left(tri_both, gcall[rows, :])
            gcall[rows, :] = jnp.where(fwd_lane, both[:DN_SUPER], both[DN_SUPER:])
            return carry

        lax.fori_loop(0, n_tot, cumulate, 0)

    def pick(x, idx):
        col = jnp.sum(jnp.where(lane == idx, x, 0.0), axis=-1, keepdims=True)
        return jnp.broadcast_to(col, x.shape)

    for d in range(2):
        al = jnp.full((1, DN_SUPER), alog_ref[d, h], F32)
        dt = jnp.full((1, DN_SUPER), dtb_ref[d, h], F32)
        for s in range(seqs):
            g_r = -jnp.exp(al) * _softplus(smallt_ref[s, 8 + d * N_HEADS + h] + dt)
            grow[d, s * n_sup:(s + 1) * n_sup, :] = _sel_right(g_r, incl[1 - d].astype(BF16))

    row_blk = lax.shift_right_logical(lax.broadcasted_iota(jnp.int32, (DN_SUPER, 1), 0), CHUNK_SHIFT)

    def stage(g, p, j):
        seq = lax.div(g, n_sup)
        base = pl.multiple_of(lax.rem(g, n_sup) * DN_SUPER, DN_SUPER)
        rows = pl.ds(pl.multiple_of(g * DN_SUPER, DN_SUPER), DN_SUPER)
        stg[p, j, 0] = l2n(conv_silu(qpad, cq_ref, seq, base)) * (HEAD_DIM ** -0.5)
        stg[p, j, 1] = l2n(conv_silu(kpad, ck_ref, seq, base))
        stg[p, j, 2] = conv_silu(vpad, cv_ref, seq, base)
        for d in range(2):
            stg[p, j, 3 + 2 * d] = pick(gcall[rows, :], 8 + d * N_HEADS + h)
            stg[p, j, 4 + 2 * d] = pick(ball[rows, :], d * N_HEADS + h)

    n_pairs = n_tot // DN_PAIR
    chains = [(j, d) for j in range(DN_PAIR) for d in range(2)]

    def prepare_pair(gg, carry):
        p = gg & 1
        nxt = jnp.minimum(gg + 1, n_pairs - 1)
        for j in range(DN_PAIR):
            stage(nxt * DN_PAIR + j, 1 - p, j)
        q = [stg[p, j, 0] for j in range(DN_PAIR)]
        k = [stg[p, j, 1] for j in range(DN_PAIR)]
        v = [stg[p, j, 2] for j in range(DN_PAIR)]
        k16 = [x.astype(BF16) for x in k]
        kk = [_dot_nt(x, x) for x in k16]
        qk_raw = [_dot_nt(x.astype(BF16), y) for x, y in zip(q, k16)]
        gc_c, be_c, decay, a_mats = {}, {}, {}, []
        for j, d in chains:
            gc_c[j, d] = stg[p, j, 3 + 2 * d]
            be_c[j, d] = stg[p, j, 4 + 2 * d]
            gc_r = grow[d, pl.ds(gg * DN_PAIR + j, 1), :]
            diff = jnp.concatenate([gc_c[j, d]] * 2, axis=1) - gc_r
            decay[j, d] = jnp.where(incl[d], jnp.exp(jnp.where(incl[d], diff, 0.0)), 0.0)
            a_mats.append(jnp.where(strict[d], kk[j] * jnp.concatenate([be_c[j, d]] * 2, axis=1) * decay[j, d], 0.0))
        t_inv = _neumann_inverses(a_mats, CHUNK)
        e_gc = {c: jnp.exp(gc_c[c]) for c in chains}
        rhs = [jnp.concatenate([v[j] * be_c[j, d], k[j] * (be_c[j, d] * e_gc[j, d])], axis=1).astype(BF16)
               for j, d in chains]
        uw16 = [_dot(t.astype(BF16), r).astype(BF16) for t, r in zip(t_inv, rhs)]
        qk16 = [(qk_raw[j] * decay[j, d]).astype(BF16) for j, d in chains]
        qu_qw = [_dot(x, y) for x, y in zip(qk16, uw16)]
        gc_last, kd_bd = [], []
        for j, d in chains:
            last = CHUNK - 1 if d == 0 else 0
            gl = [gc_c[j, d][i * CHUNK + last:i * CHUNK + last + 1, :] for i in range(per_sup)]
            gcl = jnp.concatenate([jnp.broadcast_to(x, (CHUNK, LANES)) for x in gl], axis=0)
            kd = k[j] * jnp.exp(gcl - gc_c[j, d])
            gc_last.append(gl)
            kd_bd.append(jnp.concatenate(
                [jnp.where(row_blk == i, kd, 0.0) for i in range(per_sup)], axis=1).astype(BF16))
        mn = [_dot_tn(x, y) for x, y in zip(uw16, kd_bd)]
        for n, (j, d) in enumerate(chains):
            g = gg * DN_PAIR + j
            rows = pl.ds(pl.multiple_of(g * DN_SUPER, DN_SUPER), DN_SUPER)
            for i in range(per_sup):
                c = g * per_sup + i
                cols = slice(i * HEAD_DIM, (i + 1) * HEAD_DIM)
                nt_scr[d, c] = mn[n][:HEAD_DIM, cols]
                mt_scr[d, c] = (-mn[n][HEAD_DIM:, cols]).astype(BF16)
                gl_scr[d, c] = jnp.exp(gc_last[n][i])
            qh_scr[d, rows, :] = (q[j] * e_gc[j, d] - qu_qw[n][:, HEAD_DIM:]).astype(BF16)
            if d == 1:
                of_scr[rows, :] = qu_qw[n - 1][:, :HEAD_DIM] + qu_qw[n][:, :HEAD_DIM]
        return carry

    for j in range(DN_PAIR):
        stage(j, 0, j)
    lax.fori_loop(0, n_pairs, prepare_pair, 0)

    st_scr[...] = jnp.zeros(st_scr.shape, F32)

    def scan_step(finish, i, carry):
        for s in range(seqs):
            for d in range(2):
                c = s * nc + (i if d == 0 else nc - 1 - i)
                rows = pl.ds(pl.multiple_of(c * CHUNK, CHUNK), CHUNK)
                st = st_scr[s, d]
                st16 = st.astype(BF16)
                of_scr[rows, :] += _dot_nt(qh_scr[d, rows, :], st16)
                st_scr[s, d] = st * gl_scr[d, c] + _dot(st16, mt_scr[d, c]) + nt_scr[d, c]
        if finish:
            for s in range(seqs):
                for c in (s * nc + i, s * nc + nc - 1 - i):
                    rows = pl.ds(pl.multiple_of(c * CHUNK, CHUNK), CHUNK)
                    o = of_scr[rows, :]
                    y = o * lax.rsqrt(jnp.mean(o * o, axis=-1, keepdims=True) + EPS) * nw_ref[...]
                    zz = z_ref[rows, :].astype(F32)
                    o_ref[rows, :] = (y * (zz * _sigmoid(zz))).astype(o_ref.dtype)
        return carry

    assert nc % 2 == 0
    lax.fori_loop(0, nc // 2, functools.partial(scan_step, False), 0)
    lax.fori_loop(nc // 2, nc, functools.partial(scan_step, True), 0)


def _deltanet(p_main, small, small_t, conv_w, a_log, dt_bias, norm_w, batch, s_len):
    seqs = DN_SEQS if batch % DN_SEQS == 0 else 1
    n_rows = seqs * s_len
    nc = n_rows // CHUNK
    n_sup = s_len // DN_SUPER
    assert s_len % DN_SUPER == 0 and (seqs * n_sup) % DN_PAIR == 0
    tok = lambda tile: pl.BlockSpec((n_rows, LANES), lambda b, h, tile=tile: (b, tile + h))
    cw = lambda tile: pl.BlockSpec((DN_CONV, LANES), lambda b, h, tile=tile: (0, tile + h))
    smem = pl.BlockSpec(memory_space=pltpu.SMEM)
    return pl.pallas_call(
        functools.partial(_dn_kernel, seqs),
        grid=(batch // seqs, N_HEADS),
        in_specs=[
            smem, smem,
            tok(T_Q), tok(T_K), tok(T_V), tok(T_Z),
            pl.BlockSpec((n_rows, LANES), lambda b, h: (b, 0)),
            pl.BlockSpec((seqs, N_SMALL, n_sup, DN_SUPER), lambda b, h: (b, 0, 0, 0)),
            cw(0), cw(N_HEADS), cw(2 * N_HEADS),
            pl.BlockSpec((1, LANES), lambda b, h: (0, 0)),
        ],
        out_specs=pl.BlockSpec((n_rows, LANES), lambda b, h: (b, h)),
        out_shape=jax.ShapeDtypeStruct((batch * s_len, MIX_WIDTH), ACT_DTYPE),
        scratch_shapes=[
            pltpu.VMEM((seqs, s_len + 16, LANES), F32),
            pltpu.VMEM((seqs, s_len + 16, LANES), F32),
            pltpu.VMEM((seqs, s_len + 16, LANES), F32),
            pltpu.VMEM((2, DN_PAIR, 7, DN_SUPER, LANES), F32),
            pltpu.VMEM((n_rows, LANES), F32),
            pltpu.VMEM((n_rows, LANES), F32),
            pltpu.VMEM((2, seqs * n_sup, DN_SUPER), F32),
            pltpu.VMEM((seqs, 2, HEAD_DIM, HEAD_DIM), F32),
            pltpu.VMEM((n_rows, LANES), F32),
            pltpu.VMEM((2, nc, HEAD_DIM, HEAD_DIM), BF16),
            pltpu.VMEM((2, nc, HEAD_DIM, HEAD_DIM), F32),
            pltpu.VMEM((2, n_rows, LANES), BF16),
            pltpu.VMEM((2, nc, 1, LANES), F32),
        ],
        compiler_params=pltpu.CompilerParams(
            dimension_semantics=("arbitrary", "arbitrary"), vmem_limit_bytes=VMEM_LIMIT),
        name="deltanet",
    )(a_log, dt_bias, p_main, p_main, p_main, p_main, small, small_t,
      conv_w, conv_w, conv_w, norm_w)


def _hg_kernel(layer, hq_ref, hff_ref, hfb_ref, hi_ref, hg_ref, lbl_ref, nw_ref, o_ref,
               qs, iv, lf, kk, st_scr, oi_scr, nt_scr, fl_scr, qt_scr):
    s_len = hq_ref.shape[0]
    nc = s_len // CHUNK

    lg = lbl_ref[...].astype(F32)
    e = jnp.exp(lg - jnp.max(lg, axis=0, keepdims=True))
    sm = e / jnp.sum(e, axis=0, keepdims=True)
    lb = jnp.clip(jnp.sum(sm[:layer + 1], axis=0) - sm[0], 0.0, 1.0)

    x = hq_ref[...].astype(F32)
    qs[...] = (x * _sigmoid(x)) * (HEAD_DIM ** -0.5)
    iv[...] = hi_ref[...].astype(BF16)
    for d, ref in enumerate((hff_ref, hfb_ref)):
        fr = ref[...].astype(F32)
        lbd = lb[d:d + 1, :]
        e = jnp.exp(-jnp.abs(fr))
        lf[d] = jnp.log(jnp.where(fr >= 0.0, 1.0 + lbd * e, e + lbd)) - jnp.log(1.0 + e)
        kk[d] = (1.0 - lbd) * _sigmoid(-fr)

    n_sup = s_len // HG_SUPER
    per_sup = HG_SUPER // CHUNK
    subs = CHUNK // SUB
    r_i = lax.broadcasted_iota(jnp.int32, (HG_SUPER, HG_SUPER), 0)
    c_i = lax.broadcasted_iota(jnp.int32, (HG_SUPER, HG_SUPER), 1)
    same_chunk = lax.shift_right_logical(r_i, CHUNK_SHIFT) == lax.shift_right_logical(c_i, CHUNK_SHIFT)
    r_sub = lax.shift_right_logical(r_i, SUB_SHIFT)
    c_sub = lax.shift_right_logical(c_i, SUB_SHIFT)
    causal = (r_i >= c_i, r_i <= c_i)
    row_i = lax.broadcasted_iota(jnp.int32, (HG_SUPER, 1), 0)
    row_sub = lax.shift_right_logical(row_i, SUB_SHIFT) & (subs - 1)
    row_blk = lax.shift_right_logical(row_i, CHUNK_SHIFT)
    pos16 = lax.broadcasted_iota(jnp.int32, (1, SUB, 1), 1)

    def group_row(x, group, idx):
        x3 = x.reshape(x.shape[0] // group, group, LANES)
        return jnp.broadcast_to(x3[:, idx:idx + 1, :], x3.shape).reshape(x.shape)

    chains = [(j, d) for j in range(HG_PAIR) for d in range(2)]
    tri = [(same_chunk & causal[d]).astype(BF16) for d in range(2)]
    diag_mask = [(r_sub == c_sub) & causal[d] for d in range(2)]

    def prepare_pair(gg, carry):
        sup = [gg * HG_PAIR + j for j in range(HG_PAIR)]
        rows = [pl.ds(pl.multiple_of(g * HG_SUPER, HG_SUPER), HG_SUPER) for g in sup]
        q = [qs[r, :] for r in rows]
        i16 = [iv[r, :] for r in rows]
        k = {(j, d): kk[d, rows[j], :] for j, d in chains}
        b = {(j, d): _sel_left(tri[d], lf[d, rows[j], :]) for j, d in chains}
        qh, keys, masks, expo = {}, {}, {}, {}
        for c in chains:
            j, d = c
            first = 0 if d == 0 else SUB - 1
            ref16 = group_row(b[c], SUB, first)
            expo[c] = ref16 - b[c]
            qh[c] = (q[j] * jnp.exp(b[c] - ref16)).astype(BF16)
            keys[c] = [(k[c] * jnp.exp(jnp.minimum(expo[c], HG_SAFE_EXP))).astype(BF16)]
            masks[c] = []
            for t_sub in (range(1, subs) if d == 0 else range(subs - 1)):
                ref_t = group_row(b[c], CHUNK, t_sub * SUB + first)
                src = (row_sub < t_sub) if d == 0 else (row_sub > t_sub)
                keys[c].append(
                    jnp.where(src, k[c] * jnp.exp(jnp.where(src, ref_t - b[c], 0.0)), 0.0).astype(BF16))
                masks[c].append(same_chunk & ((r_sub & (subs - 1)) == t_sub) & (
                    ((c_sub & (subs - 1)) < t_sub) if d == 0 else ((c_sub & (subs - 1)) > t_sub)))
        att_all = {c: _dot_nt(qh[c], jnp.concatenate(keys[c], axis=0)) for c in chains}
        att_off, att16 = {}, {}
        for c in chains:
            off = jnp.zeros((HG_SUPER, HG_SUPER), F32)
            for n, m in enumerate(masks[c]):
                off = off + jnp.where(m, att_all[c][:, (n + 1) * HG_SUPER:(n + 2) * HG_SUPER], 0.0)
            att_off[c] = off
            att16[c] = (off + jnp.where(diag_mask[c[1]], att_all[c][:, :HG_SUPER], 0.0)).astype(BF16)
        o_fast = {c: _dot(att16[c], i16[c[0]]) for c in chains}
        b_last, kt_bd = {}, {}
        for c in chains:
            j, d = c
            oi_scr[d, rows[j], :] = o_fast[c]
            b_last[c] = group_row(b[c], CHUNK, CHUNK - 1 if d == 0 else 0)
            qt_scr[d, rows[j], :] = (q[j] * jnp.exp(b[c])).astype(BF16)
            kt = k[c] * jnp.exp(b_last[c] - b[c])
            kt_bd[c] = jnp.concatenate(
                [jnp.where(row_blk == n, kt, 0.0) for n in range(per_sup)], axis=1).astype(BF16)
        nts = {c: _dot_tn(i16[c[0]], kt_bd[c]) for c in chains}
        for c in chains:
            j, d = c
            for n in range(per_sup):
                cc = sup[j] * per_sup + n
                nt_scr[d, cc] = nts[c][:, n * HEAD_DIM:(n + 1) * HEAD_DIM]
                fl_scr[d, cc] = jnp.exp(b_last[c][n * CHUNK:n * CHUNK + 1, :])

        for c in chains:
            j, d = c

            @pl.when(jnp.max(expo[c]) > HG_SAFE_EXP)
            def _(c=c, j=j, d=d):
                b3 = b[c].reshape(HG_SUPER // SUB, SUB, LANES)
                q3 = q[j].reshape(b3.shape)
                k3 = k[c].reshape(b3.shape)
                i3 = i16[j].astype(F32).reshape(b3.shape)
                acc = jnp.zeros(b3.shape, F32)
                for s in range(SUB):
                    ok = (pos16 >= s) if d == 0 else (pos16 <= s)
                    ex = jnp.exp(jnp.where(ok, b3 - b3[:, s:s + 1, :], 0.0))
                    col = jnp.sum(q3 * k3[:, s:s + 1, :] * ex, axis=-1, keepdims=True)
                    acc = acc + jnp.where(ok, col, 0.0) * i3[:, s:s + 1, :]
                oi_scr[d, rows[j], :] = (_dot(att_off[c].astype(BF16), i16[j])
                                         + acc.reshape(HG_SUPER, LANES))
        return carry

    lax.fori_loop(0, n_sup // HG_PAIR, prepare_pair, 0)

    st_scr[...] = jnp.zeros(st_scr.shape, F32)

    unroll = min(HG_SCAN_UNROLL, nc)
    assert nc % unroll == 0

    def scan_step(ii, carry):
        for u in range(unroll):
            i = ii * unroll + u
            for d in range(2):
                c = i if d == 0 else nc - 1 - i
                rows = pl.ds(pl.multiple_of(c * CHUNK, CHUNK), CHUNK)
                st = st_scr[d]
                oi_scr[d, rows, :] += _dot_nt(qt_scr[d, rows, :], st.astype(BF16))
                st_scr[d] = st * fl_scr[d, c] + nt_scr[d, c]
        return carry

    lax.fori_loop(0, nc // unroll, scan_step, 0)

    o = oi_scr[0] + oi_scr[1]
    y = o * lax.rsqrt(jnp.mean(o * o, axis=-1, keepdims=True) + EPS) * nw_ref[...]
    zz = hg_ref[...].astype(F32)
    o_ref[...] = (y * (zz * _sigmoid(zz))).astype(o_ref.dtype)


def _hgrn2(p_main, lb_logits, norm_w, layer, batch, s_len):
    nc = s_len // CHUNK
    assert s_len % HG_SUPER == 0
    depth = lb_logits.shape[0]
    tok = lambda tile: pl.BlockSpec((s_len, LANES), lambda b, h, tile=tile: (b, tile + h))
    return pl.pallas_call(
        functools.partial(_hg_kernel, layer),
        grid=(batch, N_HEADS),
        in_specs=[
            tok(T_HQ), tok(T_HFF), tok(T_HFB), tok(T_HI), tok(T_HGATE),
            pl.BlockSpec((depth, 2, LANES), lambda b, h: (0, 0, h)),
            pl.BlockSpec((1, LANES), lambda b, h: (0, 0)),
        ],
        out_specs=pl.BlockSpec((s_len, LANES), lambda b, h: (b, h)),
        out_shape=jax.ShapeDtypeStruct((batch * s_len, MIX_WIDTH), ACT_DTYPE),
        scratch_shapes=[
            pltpu.VMEM((s_len, LANES), F32),
            pltpu.VMEM((s_len, LANES), BF16),
            pltpu.VMEM((2, s_len, LANES), F32),
            pltpu.VMEM((2, s_len, LANES), F32),
            pltpu.VMEM((2, HEAD_DIM, HEAD_DIM), F32),
            pltpu.VMEM((2, s_len, LANES), F32),
            pltpu.VMEM((2, nc, HEAD_DIM, HEAD_DIM), F32),
            pltpu.VMEM((2, nc, 1, LANES), F32),
            pltpu.VMEM((2, s_len, LANES), BF16),
        ],
        compiler_params=pltpu.CompilerParams(
            dimension_semantics=("arbitrary", "arbitrary"), vmem_limit_bytes=VMEM_LIMIT),
        name="hgrn2",
    )(p_main, p_main, p_main, p_main, p_main, lb_logits, norm_w)


def _merge_kernel(x_ref, odn_ref, ohg_ref, gdn_ref, ghg_ref, wdn_ref, whg_ref, wo_ref, o_ref):
    a = _dot(odn_ref[...].astype(BF16), wdn_ref[...])
    b = _dot(ohg_ref[...].astype(BF16), whg_ref[...])
    merged = _sigmoid(gdn_ref[...].astype(F32)) * a + _sigmoid(ghg_ref[...].astype(F32)) * b
    o_ref[...] = x_ref[...] + _dot(merged.astype(BF16), wo_ref[...])


def _merge_out(x2d, o_dn, o_hg, p_main, w_dn, w_hg, w_out):
    t = x2d.shape[0]
    tm = min(1024, t)
    const = lambda shape: pl.BlockSpec(shape, lambda i: (0, 0))
    return pl.pallas_call(
        _merge_kernel,
        grid=(t // tm,),
        in_specs=[
            pl.BlockSpec((tm, D_MODEL), lambda i: (i, 0)),
            pl.BlockSpec((tm, MIX_WIDTH), lambda i: (i, 0)),
            pl.BlockSpec((tm, MIX_WIDTH), lambda i: (i, 0)),
            pl.BlockSpec((tm, D_MODEL), lambda i: (i, T_GATE_DN)),
            pl.BlockSpec((tm, D_MODEL), lambda i: (i, T_GATE_HG)),
            const((MIX_WIDTH, D_MODEL)), const((MIX_WIDTH, D_MODEL)), const((D_MODEL, D_MODEL)),
        ],
        out_specs=pl.BlockSpec((tm, D_MODEL), lambda i: (i, 0)),
        out_shape=jax.ShapeDtypeStruct((t, D_MODEL), F32),
        compiler_params=pltpu.CompilerParams(
            dimension_semantics=("arbitrary",), vmem_limit_bytes=VMEM_LIMIT),
        name="merge_out",
    )(x2d, o_dn, o_hg, p_main, p_main, w_dn, w_hg, w_out)


FF_COLS = D_FF // 2


def _ffn_kernel(final, xm_ref, xp_ref, xn_ref, nw_ref, wg_ref, wu_ref, cw_ref, cb_ref, wd_ref, fn_ref,
                o_ref, g_scr):
    i = pl.program_id(1)
    last = pl.num_programs(1) - 1
    ts = xm_ref.shape[1]
    x = xm_ref[0]
    h_ext = _rms(jnp.concatenate([xp_ref[0], x, xn_ref[0]], axis=0), nw_ref[...])
    h = h_ext[8:8 + ts].astype(BF16)
    h_ext = h_ext.astype(BF16)
    acc = x
    for c in range(D_FF // FF_COLS):
        cols = slice(c * FF_COLS, (c + 1) * FF_COLS)
        g_scr[...] = _dot(h_ext, wg_ref[:, cols])
        g_scr[7:8, :] = jnp.where(i > 0, g_scr[7:8, :], 0.0)
        g_scr[8 + ts:9 + ts, :] = jnp.where(i < last, g_scr[8 + ts:9 + ts, :], 0.0)
        gc = (cw_ref[0:1, cols] * g_scr[7:7 + ts, :] + cw_ref[1:2, cols] * g_scr[8:8 + ts, :]
              + cw_ref[2:3, cols] * g_scr[9:9 + ts, :] + cb_ref[:, cols])
        u = _dot(h, wu_ref[:, cols])
        act = (gc * _sigmoid(gc)) * u
        acc = acc + _dot(act.astype(BF16), wd_ref[cols, :])
    if final:
        acc = _rms(acc, fn_ref[...])
    o_ref[0] = acc


def _ffn(x3d, norm_w, w_gate, w_upp, conv_w, conv_b, w_down, final_w, final):
    batch, s_len, _ = x3d.shape
    ts = min(512, s_len)
    r8 = ts // 8
    n8 = s_len // 8
    const = lambda shape: pl.BlockSpec(shape, lambda b, i: (0, 0), pipeline_mode=pl.Buffered(1))
    return pl.pallas_call(
        functools.partial(_ffn_kernel, final),
        grid=(batch, s_len // ts),
        in_specs=[
            pl.BlockSpec((1, ts, D_MODEL), lambda b, i: (b, i, 0)),
            pl.BlockSpec((1, 8, D_MODEL), lambda b, i: (b, jnp.maximum(i * r8 - 1, 0), 0)),
            pl.BlockSpec((1, 8, D_MODEL), lambda b, i: (b, jnp.minimum((i + 1) * r8, n8 - 1), 0)),
            const((1, D_MODEL)),
            const((D_MODEL, D_FF)), const((D_MODEL, D_FF)),
            const((3, D_FF)), const((1, D_FF)),
            const((D_FF, D_MODEL)),
            const((1, D_MODEL)),
        ],
        out_specs=pl.BlockSpec((1, ts, D_MODEL), lambda b, i: (b, i, 0)),
        out_shape=jax.ShapeDtypeStruct(x3d.shape, F32),
        scratch_shapes=[pltpu.VMEM((ts + 16, FF_COLS), F32)],
        compiler_params=pltpu.CompilerParams(
            dimension_semantics=("arbitrary", "arbitrary"), vmem_limit_bytes=VMEM_LIMIT),
        name="ffn",
    )(x3d, x3d, x3d, norm_w, w_gate, w_upp, conv_w, conv_b, w_down, final_w)


def kernel(x, mix_norm, w_in, dn_conv, dn_a_log, dn_dt_bias, dn_norm, hg_lb_logits, hg_norm,
           w_branch_dn, w_branch_hg, w_out, ffn_norm, w_up, ffn_conv, ffn_conv_bias, w_down, final_norm):
    batch, s_len, _ = x.shape
    depth = mix_norm.shape[0]
    x2d = x.reshape(batch * s_len, D_MODEL)
    final_w = final_norm.reshape(1, D_MODEL)
    for l in range(depth):
        w = w_in[l]
        w_main = jnp.concatenate(
            [w[:, _C_GATE:], w[:, :_C_Z_END], w[:, _C_HQ:_C_GATE]], axis=1).astype(BF16)
        w_s = w[:, _C_Z_END:_C_HQ].astype(BF16)
        w_small = jnp.pad(w_s, ((0, 0), (0, LANES - N_SMALL)))
        p_main, small, small_t = _inproj(x2d, mix_norm[l].reshape(1, D_MODEL), w_main, w_small, w_s.T)
        small_t = small_t.reshape(N_SMALL, batch, s_len // DN_SUPER, DN_SUPER).transpose(1, 0, 2, 3)
        o_dn = _deltanet(p_main, small, small_t, dn_conv[l], dn_a_log[l], dn_dt_bias[l],
                         dn_norm[l].reshape(1, HEAD_DIM), batch, s_len)
        o_hg = _hgrn2(p_main, hg_lb_logits, hg_norm[l].reshape(1, HEAD_DIM), l, batch, s_len)
        x2d = _merge_out(x2d, o_dn, o_hg, p_main, w_branch_dn[l].astype(BF16),
                         w_branch_hg[l].astype(BF16), w_out[l].astype(BF16))
        wu = w_up[l].astype(BF16)
        x3d = _ffn(x2d.reshape(batch, s_len, D_MODEL), ffn_norm[l].reshape(1, D_MODEL),
                   wu[:, :D_FF], wu[:, D_FF:], ffn_conv[l], ffn_conv_bias[l].reshape(1, D_FF),
                   w_down[l].astype(BF16), final_w, l == depth - 1)
        x2d = x3d.reshape(batch * s_len, D_MODEL)
    return x2d.reshape(batch, s_len, D_MODEL)
```
